```python
import jax
import jax.numpy as jnp
from jax import lax
import numpy as np

D_MODEL = 4096
BATCH = 4
SEQ = 2048
DEPTH = 2
DEC_BATCH = 8
DEC_SEQ = 8
PAST_LEN = 16384
PAGE_SIZE = 128

N_META = 16
RW_HEAD = 64
RW_HEADS = 32
RW_WIDTH = RW_HEADS * RW_HEAD
W_LORA = 128
A_LORA = 128
G_LORA = 480
RW_COLS = 3 * RW_WIDTH + W_LORA + A_LORA + G_LORA
ATT_HEADS = 16
ATT_HEAD = 128
ATT_WIDTH = ATT_HEADS * ATT_HEAD
IDX_HEADS = 16
IDX_DIM = 64
TOPK_MAX = 256
Q_BLOCK = 64
ATT_COLS = 3 * ATT_WIDTH + IDX_HEADS * IDX_DIM + IDX_DIM + IDX_HEADS
GATE_COLS = 2 * D_MODEL
N_COLS = RW_COLS + ATT_COLS + GATE_COLS
D_FF = 11008
CONV_W = 3
RMS_EPS = 1e-6
GN_EPS = 64e-5
ATT_SCALE = ATT_HEAD ** -0.5
IDX_SCALE = (IDX_HEADS ** -0.5) * (IDX_DIM ** -0.5)

kernel_name = 'hybrid_rwkv7_dsa_convffn_step'

F32 = jnp.float32


def rmsnorm(x, g):
    xf = x.astype(F32)
    y = xf * lax.rsqrt(jnp.mean(xf * xf, axis=-1, keepdims=True) + RMS_EPS)
    return (y * g.astype(F32)).astype(x.dtype)


def split_cols(a, sizes):
    return jnp.split(a, [int(s) for s in np.cumsum(sizes)[:-1]], axis=-1)


def rwkv7_mixer(rw, shift_prev, wkv_prev, l, p):
    B, T, _ = rw.shape
    rw_prev = jnp.concatenate([shift_prev.astype(rw.dtype), rw[:, :-1]], axis=1)
    xs = rw + (rw_prev - rw) * p['rw_mu'][l]
    r, k, v, wd, ad, gd = split_cols(xs, [RW_WIDTH, RW_WIDTH, RW_WIDTH, W_LORA, A_LORA, G_LORA])
    w_log = -jax.nn.softplus(-(p['rw_w0'][l] + jnp.tanh(wd) @ p['rw_w2'][l])) - 0.5
    decay = jnp.exp(-jnp.exp(w_log.astype(F32)))
    a = jax.nn.sigmoid(p['rw_a0'][l] + ad @ p['rw_a2'][l])
    g = jax.nn.sigmoid(gd) @ p['rw_g2'][l]
    kk = k * p['rw_k_k'][l]
    k = k * (1.0 + (a - 1.0) * p['rw_k_a'][l])
    heads = lambda t: t.astype(F32).reshape(B, T, RW_HEADS, RW_HEAD)
    r, k, v, kk, a, decay = heads(r), heads(k), heads(v), heads(kk), heads(a), heads(decay)
    kk = kk / jnp.maximum(jnp.sqrt(jnp.sum(kk * kk, axis=-1, keepdims=True)), 1e-12)

    def step(S, inp):
        r_t, w_t, k_t, v_t, kk_t, a_t = inp
        sa = jnp.einsum('bhij,bhj->bhi', S, -kk_t)
        S = (S * w_t[:, :, None, :] + sa[..., None] * (kk_t * a_t)[:, :, None, :]
             + v_t[..., None] * k_t[:, :, None, :])
        return S, jnp.einsum('bhij,bhj->bhi', S, r_t)

    seq = tuple(jnp.moveaxis(t, 1, 0) for t in (r, decay, k, v, kk, a))
    wkv_new, y = lax.scan(step, wkv_prev.astype(F32), seq)
    y = jnp.moveaxis(y, 0, 1)
    mean = jnp.mean(y, axis=-1, keepdims=True)
    var = jnp.mean(jnp.square(y - mean), axis=-1, keepdims=True)
    y = (y - mean) * lax.rsqrt(var + GN_EPS)
    y = (y * p['rw_ln_w'][l].astype(F32).reshape(RW_HEADS, RW_HEAD)
         + p['rw_ln_b'][l].astype(F32).reshape(RW_HEADS, RW_HEAD))
    y = y + jnp.sum(r * k * p['rw_r_k'][l].astype(F32), axis=-1, keepdims=True) * v
    y = y.reshape(B, T, RW_WIDTH).astype(rw.dtype) * g
    return y, rw[:, -1:], wkv_new


def dsa_block(q, qi, wi, q_pos, k_idx, gather_rows):
    L = k_idx.shape[1]
    topk = min(TOPK_MAX, L // 4)
    dots = jnp.einsum('bqhd,bsd->bqhs', qi.astype(F32), k_idx.astype(F32))
    score = jnp.einsum('bqh,bqhs->bqs', wi.astype(F32) * IDX_SCALE, jax.nn.relu(dots))
    causal = jnp.arange(L, dtype=jnp.int32)[None, :] <= q_pos[:, None]
    score = jnp.where(causal[None], score, -jnp.inf)
    top_val, idx = lax.top_k(score, topk)
    valid = jnp.isfinite(top_val)
    k_sel, v_sel = gather_rows(idx)
    logits = jnp.einsum('bqhd,bqkhd->bhqk', q.astype(F32), k_sel.astype(F32)) * ATT_SCALE
    logits = jnp.where(valid[:, None], logits, -jnp.inf)
    prob = jax.nn.softmax(logits, axis=-1)
    return jnp.einsum('bhqk,bqkhd->bqhd', prob.astype(v_sel.dtype), v_sel)


def prompt_attend(q, k, v, qi, ki, wi):
    B, T = q.shape[:2]
    nblk = -(-T // Q_BLOCK)
    Tp = nblk * Q_BLOCK

    def blk(a):
        a = jnp.pad(a, [(0, 0), (0, Tp - T)] + [(0, 0)] * (a.ndim - 2))
        return jnp.moveaxis(a.reshape((B, nblk, Q_BLOCK) + a.shape[2:]), 1, 0)

    pos = jnp.arange(Tp, dtype=jnp.int32).reshape(nblk, Q_BLOCK)
    bidx = jnp.arange(B)[:, None, None]

    def gather_rows(idx):
        return k[bidx, idx], v[bidx, idx]

    def one_block(args):
        qb, qib, wib, pb = args
        return dsa_block(qb, qib, wib, pb, ki, gather_rows)

    o = lax.map(one_block, (blk(q), blk(qi), blk(wi), pos))
    return jnp.moveaxis(o, 0, 1).reshape((B, Tp) + q.shape[2:])[:, :T]


def make_sample_attend(l, cache_k, cache_v, cache_kidx, page_table):
    def attend(q, k, v, qi, ki, wi):
        DB, DS = q.shape[:2]
        past = page_table.shape[1] * PAGE_SIZE
        kidx_past = cache_kidx[l, page_table].reshape(DB, past, IDX_DIM)
        kidx_all = jnp.concatenate([kidx_past.astype(ki.dtype), ki], axis=1)
        bidx = jnp.arange(DB)[:, None, None]

        def gather_rows(idx):
            in_past = (idx < past)[..., None, None]
            pidx = jnp.minimum(idx, past - 1)
            phys = page_table[bidx, pidx // PAGE_SIZE]
            off = pidx % PAGE_SIZE
            nidx = jnp.clip(idx - past, 0, DS - 1)
            ks = jnp.where(in_past, cache_k[l, phys, off].astype(k.dtype), k[bidx, nidx])
            vs = jnp.where(in_past, cache_v[l, phys, off].astype(v.dtype), v[bidx, nidx])
            return ks, vs

        pos = past + jnp.arange(DS, dtype=jnp.int32)
        return dsa_block(q, qi, wi, pos, kidx_all, gather_rows)
    return attend


def trunk_layer(x, l, p, shift_prev, wkv_prev, conv_prev, attend):
    B, T, _ = x.shape
    h = rmsnorm(x, p['norm_mix'][l])
    cols = h @ p['w_in'][l]
    rw, att, gates = split_cols(cols, [RW_COLS, ATT_COLS, GATE_COLS])
    y_a, shift_new, wkv_new = rwkv7_mixer(rw, shift_prev, wkv_prev, l, p)
    q, k, v, qi, ki, wi = split_cols(att, [ATT_WIDTH, ATT_WIDTH, ATT_WIDTH, IDX_HEADS * IDX_DIM, IDX_DIM, IDX_HEADS])
    q = q.reshape(B, T, ATT_HEADS, ATT_HEAD)
    k = k.reshape(B, T, ATT_HEADS, ATT_HEAD)
    v = v.reshape(B, T, ATT_HEADS, ATT_HEAD)
    qi = qi.reshape(B, T, IDX_HEADS, IDX_DIM)
    ki = rmsnorm(ki, p['idx_k_norm'][l])
    y_b = attend(q, k, v, qi, ki, wi).reshape(B, T, ATT_WIDTH)
    g_a, g_b = split_cols(gates, [D_MODEL, D_MODEL])
    mix = (jax.nn.sigmoid(g_a) * (y_a @ p['w_branch_a'][l])
           + jax.nn.sigmoid(g_b) * (y_b @ p['w_branch_b'][l]))
    x = x + mix @ p['w_out'][l]
    h2 = rmsnorm(x, p['norm_ffn'][l])
    u = h2 @ p['w_up'][l]
    ucat = jnp.concatenate([conv_prev.astype(u.dtype), u], axis=1)
    cw = p['conv_w'][l]
    uc = sum(cw[j] * ucat[:, j:j + T] for j in range(CONV_W)) + p['conv_b'][l]
    conv_new = ucat[:, T:]
    gate, val = split_cols(uc, [D_FF, D_FF])
    x = x + (jax.nn.silu(gate) * val) @ p['w_down'][l]
    return x, (k, v, ki, wkv_new, shift_new, conv_new)


def setup_inputs(seed: int = 0) -> dict:
    key = jax.random.key(seed)
    ks = iter(jax.random.split(key, 48))
    nrm = lambda shape, scale: jax.random.normal(next(ks), shape, jnp.float32) * scale
    gain = lambda shape: 1.0 + nrm(shape, 0.01)
    n_pages = PAST_LEN // PAGE_SIZE
    n_used = DEC_BATCH * n_pages
    n_pool = n_used + max(1, n_used // 4)
    page_table = jax.random.permutation(next(ks), n_pool)[:n_used].reshape(DEC_BATCH, n_pages).astype(jnp.int32)
    return {
        'x_prompt': nrm((BATCH, SEQ, D_MODEL), 1.0),
        'x_sample': nrm((DEC_BATCH, DEC_SEQ, D_MODEL), 1.0),
        'cache_k': nrm((DEPTH, n_pool, PAGE_SIZE, ATT_HEADS, ATT_HEAD), 1.0),
        'cache_v': nrm((DEPTH, n_pool, PAGE_SIZE, ATT_HEADS, ATT_HEAD), 1.0),
        'cache_kidx': nrm((DEPTH, n_pool, PAGE_SIZE, IDX_DIM), 1.0),
        'state_wkv': nrm((DEPTH, DEC_BATCH, RW_HEADS, RW_HEAD, RW_HEAD), 0.1),
        'state_shift': nrm((DEPTH, DEC_BATCH, 1, RW_COLS), 1.0),
        'state_conv': nrm((DEPTH, DEC_BATCH, CONV_W - 1, 2 * D_FF), 1.0),
        'page_table': page_table,
        'meta_tokens': nrm((N_META, D_MODEL), 1.0),
        'norm_mix': gain((DEPTH, D_MODEL)),
        'w_in': nrm((DEPTH, D_MODEL, N_COLS), D_MODEL ** -0.5),
        'rw_mu': jax.random.uniform(next(ks), (DEPTH, RW_COLS), jnp.float32),
        'rw_w0': jax.random.uniform(next(ks), (DEPTH, RW_WIDTH), jnp.float32, -6.0, -1.0),
        'rw_w2': nrm((DEPTH, W_LORA, RW_WIDTH), 0.1 * W_LORA ** -0.5),
        'rw_a0': nrm((DEPTH, RW_WIDTH), 0.1),
        'rw_a2': nrm((DEPTH, A_LORA, RW_WIDTH), A_LORA ** -0.5),
        'rw_g2': nrm((DEPTH, G_LORA, RW_WIDTH), G_LORA ** -0.5),
        'rw_k_k': 0.85 + nrm((DEPTH, RW_WIDTH), 0.02),
        'rw_k_a': 1.0 + nrm((DEPTH, RW_WIDTH), 0.02),
        'rw_r_k': nrm((DEPTH, RW_HEADS, RW_HEAD), 0.1),
        'rw_ln_w': gain((DEPTH, RW_WIDTH)),
        'rw_ln_b': nrm((DEPTH, RW_WIDTH), 0.01),
        'idx_k_norm': gain((DEPTH, IDX_DIM)),
        'w_branch_a': nrm((DEPTH, RW_WIDTH, D_MODEL), RW_WIDTH ** -0.5),
        'w_branch_b': nrm((DEPTH, ATT_WIDTH, D_MODEL), ATT_WIDTH ** -0.5),
        'w_out': nrm((DEPTH, D_MODEL, D_MODEL), D_MODEL ** -0.5),
        'norm_ffn': gain((DEPTH, D_MODEL)),
        'w_up': nrm((DEPTH, D_MODEL, 2 * D_FF), D_MODEL ** -0.5),
        'conv_w': nrm((DEPTH, CONV_W, 2 * D_FF), CONV_W ** -0.5),
        'conv_b': nrm((DEPTH, 2 * D_FF), 0.01),
        'w_down': nrm((DEPTH, D_FF, D_MODEL), D_FF ** -0.5),
        'norm_final': gain((D_MODEL,)),
    }


def reference(x_prompt, x_sample, cache_k, cache_v, cache_kidx, state_wkv, state_shift, state_conv,
              page_table, meta_tokens, norm_mix, w_in, rw_mu, rw_w0, rw_w2, rw_a0, rw_a2, rw_g2,
              rw_k_k, rw_k_a, rw_r_k, rw_ln_w, rw_ln_b, idx_k_norm, w_branch_a, w_branch_b, w_out,
              norm_ffn, w_up, conv_w, conv_b, w_down, norm_final):
    p = {'norm_mix': norm_mix, 'w_in': w_in, 'rw_mu': rw_mu, 'rw_w0': rw_w0, 'rw_w2': rw_w2,
         'rw_a0': rw_a0, 'rw_a2': rw_a2, 'rw_g2': rw_g2, 'rw_k_k': rw_k_k, 'rw_k_a': rw_k_a,
         'rw_r_k': rw_r_k, 'rw_ln_w': rw_ln_w, 'rw_ln_b': rw_ln_b, 'idx_k_norm': idx_k_norm,
         'w_branch_a': w_branch_a, 'w_branch_b': w_branch_b, 'w_out': w_out, 'norm_ffn': norm_ffn,
         'w_up': w_up, 'conv_w': conv_w, 'conv_b': conv_b, 'w_down': w_down}
    B = x_prompt.shape[0]
    meta = jnp.broadcast_to(meta_tokens.astype(x_prompt.dtype)[None], (B, N_META, D_MODEL))
    xp = jnp.concatenate([meta, x_prompt], axis=1)
    xs = x_sample
    zero_shift = jnp.zeros((B, 1, RW_COLS), x_prompt.dtype)
    zero_wkv = jnp.zeros((B, RW_HEADS, RW_HEAD, RW_HEAD), F32)
    zero_conv = jnp.zeros((B, CONV_W - 1, 2 * D_FF), x_prompt.dtype)
    sp, ss = [], []
    for l in range(DEPTH):
        xp, st = trunk_layer(xp, l, p, zero_shift, zero_wkv, zero_conv, prompt_attend)
        sp.append(st)
        xs, st = trunk_layer(xs, l, p, state_shift[l], state_wkv[l], state_conv[l],
                             make_sample_attend(l, cache_k, cache_v, cache_kidx, page_table))
        ss.append(st)
    y_prompt = rmsnorm(xp, norm_final)[:, N_META:]
    y_sample = rmsnorm(xs, norm_final)
    stk = lambda sts, i: jnp.stack([s[i] for s in sts])
    return (y_prompt, y_sample,
            stk(sp, 0), stk(sp, 1), stk(sp, 2), stk(sp, 3), stk(sp, 4), stk(sp, 5),
            stk(ss, 0), stk(ss, 1), stk(ss, 2), stk(ss, 3), stk(ss, 4), stk(ss, 5))
```

```python
import functools

import jax
import jax.numpy as jnp
from jax import lax
from jax.experimental import pallas as pl
from jax.experimental.pallas import tpu as pltpu

F32 = jnp.float32
BF16 = jnp.bfloat16
I32 = jnp.int32
HIGHEST = lax.Precision.HIGHEST

D_MODEL = 4096
N_META = 16
PAGE_SIZE = 128
RW_HEAD = 64
RW_HEADS = 32
RW_WIDTH = RW_HEADS * RW_HEAD
W_LORA = 128
A_LORA = 128
G_LORA = 480
RW_COLS = 3 * RW_WIDTH + W_LORA + A_LORA + G_LORA
ATT_HEADS = 16
ATT_HEAD = 128
ATT_WIDTH = ATT_HEADS * ATT_HEAD
IDX_HEADS = 16
IDX_DIM = 64
TOPK_MAX = 256
ATT_COLS = 3 * ATT_WIDTH + IDX_HEADS * IDX_DIM + IDX_DIM + IDX_HEADS
D_FF = 11008
CONV_W = 3
RMS_EPS = 1e-6
GN_EPS = 64e-5
ATT_SCALE = ATT_HEAD ** -0.5
IDX_SCALE = (IDX_HEADS ** -0.5) * (IDX_DIM ** -0.5)

LANES = 128
RW_PAD = 6912
IDX_PAD = 1152
G_PAD = 512
INT_MIN = -2 ** 31
NEG_INF_KEY = INT_MIN + 0x7FFFFF
BIG_COL = 2 ** 30
VMEM_MB = 60


def _cparams(sem, mb=VMEM_MB):
    return pltpu.CompilerParams(dimension_semantics=sem, vmem_limit_bytes=mb * 1024 * 1024)


def _dot(a, b):
    return jnp.dot(a.astype(BF16), b.astype(BF16), preferred_element_type=F32)


def _dot_nt(a, b, precision=None):
    if precision is None:
        a, b = a.astype(BF16), b.astype(BF16)
    return lax.dot_general(a, b, (((1,), (1,)), ((), ())), precision=precision,
                           preferred_element_type=F32)


def _dot_hi(a, b):
    return jnp.dot(a, b, precision=HIGHEST, preferred_element_type=F32)


def _rms_kernel(x_ref, g_ref, o_ref):
    x = x_ref[...]
    ms = jnp.sum(x * x, axis=-1, keepdims=True) * (1.0 / x.shape[-1])
    o_ref[...] = ((x * lax.rsqrt(ms + RMS_EPS)) * g_ref[...]).astype(o_ref.dtype)


def rmsnorm_rows(x, g, *, tm, out_dtype):
    m, d = x.shape
    return pl.pallas_call(
        _rms_kernel,
        grid=(m // tm,),
        in_specs=[pl.BlockSpec((tm, d), lambda i: (i, 0)), pl.BlockSpec((1, d), lambda i: (0, 0))],
        out_specs=pl.BlockSpec((tm, d), lambda i: (i, 0)),
        out_shape=jax.ShapeDtypeStruct((m, d), out_dtype),
        compiler_params=_cparams(("parallel",)),
    )(x, g.reshape(1, d))


def _mm_kernel(x_ref, w_ref, o_ref):
    o_ref[...] = _dot(x_ref[...], w_ref[...]).astype(o_ref.dtype)


def _mm_res_kernel(x_ref, w_ref, r_ref, o_ref):
    o_ref[...] = (r_ref[...] + _dot(x_ref[...], w_ref[...])).astype(o_ref.dtype)


def matmul(x, w, *, tm, tn, out_dtype=F32, residual=None):
    m, k = x.shape
    n = w.shape[1]
    assert m % tm == 0 and n % tn == 0, (m, tm, n, tn)
    in_specs = [pl.BlockSpec((tm, k), lambda i, j: (i, 0)), pl.BlockSpec((k, tn), lambda i, j: (0, j))]
    args = [x, w]
    kern = _mm_kernel
    if residual is not None:
        in_specs.append(pl.BlockSpec((tm, tn), lambda i, j: (i, j)))
        args.append(residual)
        kern = _mm_res_kernel
    return pl.pallas_call(
        kern,
        grid=(m // tm, n // tn),
        in_specs=in_specs,
        out_specs=pl.BlockSpec((tm, tn), lambda i, j: (i, j)),
        out_shape=jax.ShapeDtypeStruct((m, n), out_dtype),
        compiler_params=_cparams(("parallel", "arbitrary")),
    )(*args)


def _mix_kernel(ya_ref, yb_ref, wa_ref, wb_ref, ga_ref, gb_ref, o_ref):
    pa = _dot(ya_ref[...], wa_ref[...])
    pb = _dot(yb_ref[...], wb_ref[...])
    o_ref[...] = (jax.nn.sigmoid(ga_ref[...]) * pa + jax.nn.sigmoid(gb_ref[...]) * pb).astype(o_ref.dtype)


def branch_mix(ya, yb, wa, wb, gates, *, tm, tn):
    m, k = ya.shape
    n = wa.shape[1]
    assert m % tm == 0 and n % tn == 0, (m, tm, n, tn)
    nb = n // tn
    return pl.pallas_call(
        _mix_kernel,
        grid=(m // tm, nb),
        in_specs=[pl.BlockSpec((tm, k), lambda i, j: (i, 0)),
                  pl.BlockSpec((tm, k), lambda i, j: (i, 0)),
                  pl.BlockSpec((k, tn), lambda i, j: (0, j)),
                  pl.BlockSpec((k, tn), lambda i, j: (0, j)),
                  pl.BlockSpec((tm, tn), lambda i, j: (i, j)),
                  pl.BlockSpec((tm, tn), lambda i, j: (i, j + nb))],
        out_specs=pl.BlockSpec((tm, tn), lambda i, j: (i, j)),
        out_shape=jax.ShapeDtypeStruct((m, n), BF16),
        compiler_params=_cparams(("parallel", "arbitrary")),
    )(ya, yb, wa, wb, gates, gates)


def _shift_rows(x, first_rows, n):
    rolled = pltpu.roll(x, n, 0)
    row = lax.broadcasted_iota(I32, x.shape, 0)
    out = rolled
    for j in range(n):
        out = jnp.where(row == j, first_rows[j:j + 1, :], out)
    return out


def _rw_prep_kernel(x_ref, sp_ref, mu_ref, w0_ref, w2_ref, a0_ref, a2_ref, g2_ref, kk_ref, ka_ref,
                    r_o, lw_o, k_o, v_o, kk_o, a_o, g_o, carry):
    t = pl.program_id(1)
    tt = x_ref.shape[0]

    @pl.when(t == 0)
    def _():
        carry[...] = sp_ref[0]

    x = x_ref[...]
    xprev = _shift_rows(x, carry[...], 1)
    carry[...] = x[tt - 1:tt, :]
    xs = x + (xprev - x) * mu_ref[...]
    w = RW_WIDTH
    r = xs[:, 0:w]
    k = xs[:, w:2 * w]
    v = xs[:, 2 * w:3 * w]
    wd = xs[:, 3 * w:3 * w + W_LORA]
    ad = xs[:, 3 * w + W_LORA:3 * w + W_LORA + A_LORA]
    gd = xs[:, 3 * w + W_LORA + A_LORA:RW_PAD]
    z = -(w0_ref[...] + _dot(jnp.tanh(wd), w2_ref[...]))
    softplus = jnp.maximum(z, 0.0) + jnp.log(1.0 + jnp.exp(-jnp.abs(z)))
    w_log = -softplus - 0.5
    a = jax.nn.sigmoid(a0_ref[...] + _dot(ad, a2_ref[...]))
    r_o[...] = r
    lw_o[...] = -jnp.exp(w_log)
    k_o[...] = k * (1.0 + (a - 1.0) * ka_ref[...])
    v_o[...] = v
    kk_o[...] = k * kk_ref[...]
    a_o[...] = a
    g_o[...] = _dot(jax.nn.sigmoid(gd), g2_ref[...])


def rw_prep(rw, shift_prev, mu, w0, w2, a0, a2, g2, k_k, k_a, *, nseq, seqlen, tt):
    nt = seqlen // tt
    row = lambda b, t: (b * nt + t, 0)
    const = lambda b, t: (0, 0)
    vec = lambda n: pl.BlockSpec((1, n), const)
    out_sds = jax.ShapeDtypeStruct((nseq * seqlen, RW_WIDTH), F32)
    return pl.pallas_call(
        _rw_prep_kernel,
        grid=(nseq, nt),
        in_specs=[pl.BlockSpec((tt, RW_PAD), row),
                  pl.BlockSpec((1, 1, RW_PAD), lambda b, t: (b, 0, 0)),
                  vec(RW_PAD), vec(RW_WIDTH),
                  pl.BlockSpec((W_LORA, RW_WIDTH), const), vec(RW_WIDTH),
                  pl.BlockSpec((A_LORA, RW_WIDTH), const),
                  pl.BlockSpec((G_PAD, RW_WIDTH), const), vec(RW_WIDTH), vec(RW_WIDTH)],
        out_specs=[pl.BlockSpec((tt, RW_WIDTH), row)] * 7,
        out_shape=[out_sds] * 7,
        scratch_shapes=[pltpu.VMEM((1, RW_PAD), F32)],
        compiler_params=_cparams(("arbitrary", "arbitrary"), 48),
    )(rw, shift_prev, mu, w0, w2, a0, a2, g2, k_k, k_a)


def _pad_rows(x, rows):
    if x.shape[0] == rows:
        return x
    return jnp.concatenate([x, jnp.zeros((rows - x.shape[0], x.shape[1]), x.dtype)], axis=0)


def _scan_kernel(r_ref, lw_ref, k_ref, v_ref, kk_ref, a_ref, g_ref, rk_ref, lnw_ref, lnb_ref, s0_ref,
                 y_ref, so_ref, s_scr, *, chunk, hpb):
    c = chunk
    c2 = 2 * c
    ci = pl.program_id(2)

    @pl.when(ci == 0)
    def _():
        s_scr[...] = s0_ref[0]

    lane = lax.broadcasted_iota(I32, (c, LANES), 1)
    m0 = lane < RW_HEAD
    ei = lax.broadcasted_iota(I32, (LANES, LANES), 0)
    ej = lax.broadcasted_iota(I32, (LANES, LANES), 1)
    same_head = lax.shift_right_logical(ei, 6) == lax.shift_right_logical(ej, 6)
    e_sum = jnp.where(same_head, 1.0, 0.0).astype(F32)
    e_mean = e_sum * (1.0 / RW_HEAD)
    ti = lax.broadcasted_iota(I32, (c, c), 0)
    tj = lax.broadcasted_iota(I32, (c, c), 1)
    tril_incl = jnp.where(ti >= tj, 1.0, 0.0).astype(F32)
    ri = lax.broadcasted_iota(I32, (c2, LANES), 0)
    cj = lax.broadcasted_iota(I32, (c2, LANES), 1)
    rl = jnp.where(ri >= c, ri - c, ri)
    cl = jnp.where(cj >= c, cj - c, cj)
    same_blk = (jnp.where(ri >= c, 1, 0) == jnp.where(cj >= c, 1, 0)) & (cj < c2)
    strict = same_blk & (rl > cl)
    incl = same_blk & (rl >= cl)
    zeros_c = jnp.zeros((c, LANES), F32)
    n_steps = max(1, (c - 1).bit_length())

    for p in range(hpb):
        sl = slice(p * LANES, (p + 1) * LANES)
        r = r_ref[:, sl]
        lw = lw_ref[:, sl]
        k = k_ref[:, sl]
        v = v_ref[:, sl]
        kkr = kk_ref[:, sl]
        a = a_ref[:, sl]
        s = s_scr[p]

        ss = _dot_hi(kkr * kkr, e_sum)
        kk = kkr / jnp.maximum(jnp.sqrt(ss), 1e-12)
        b = kk * a
        cum = _dot_hi(tril_incl, lw)
        p_incl = jnp.exp(cum)
        p_excl = jnp.exp(cum - lw)
        p_inv = jnp.exp(-cum)
        at = -kk * p_excl
        rt = r * p_incl
        bt = b * p_inv
        kt = k * p_inv

        lhs4 = jnp.concatenate([jnp.where(m0, at, 0.0), jnp.where(m0, 0.0, at),
                                jnp.where(m0, rt, 0.0), jnp.where(m0, 0.0, rt)], axis=0)
        m1b = _dot_nt(lhs4, _pad_rows(jnp.concatenate([bt, bt], axis=0), LANES))
        m1k = _dot_nt(lhs4, _pad_rows(jnp.concatenate([kt, kt], axis=0), LANES))
        m2 = _dot_nt(jnp.concatenate([at, rt], axis=0), s)
        v2 = _pad_rows(jnp.concatenate([v, v], axis=0), LANES)
        lak = jnp.where(strict, m1k[:c2], 0.0)
        rhs = jnp.concatenate([m2[:c], m2[:c]], axis=0) + _dot(lak, v2)
        u = _pad_rows(rhs, LANES)
        pw = _pad_rows(jnp.where(strict, m1b[:c2], 0.0), LANES)
        for step in range(n_steps):
            u = u + _dot(pw, u)
            if step + 1 < n_steps:
                pw = _dot(pw, pw)
        u_pair = jnp.where(m0, u[:c], u[c:c2])
        u2 = _pad_rows(jnp.concatenate([u_pair, u_pair], axis=0), LANES)
        yst = (_dot(jnp.where(incl, m1b[c2:], 0.0), u2) + _dot(jnp.where(incl, m1k[c2:], 0.0), v2))
        y = m2[c:] + jnp.where(m0, yst[:c], yst[c:])

        p_end = p_incl[c - 1:c, :]
        xt = _pad_rows(jnp.concatenate([u_pair, v], axis=0), LANES)
        yk = _pad_rows(jnp.concatenate([bt * p_end, kt * p_end], axis=0), LANES)
        s_new = s * p_end + _dot(xt.T, yk)
        s_scr[p] = jnp.where(same_head, s_new, 0.0)

        mean = _dot_hi(y, e_mean)
        d = y - mean
        var = _dot_hi(d * d, e_mean)
        yn = d * lax.rsqrt(var + GN_EPS) * lnw_ref[:, sl] + lnb_ref[:, sl]
        bonus = _dot_hi(r * k * rk_ref[:, sl], e_sum)
        y_ref[:, sl] = ((yn + bonus * v) * g_ref[:, sl]).astype(y_ref.dtype)

    @pl.when(ci == pl.num_programs(2) - 1)
    def _():
        so_ref[0] = s_scr[...]


def rw_scan(r, lw, k, v, kk, a, g, r_k, ln_w, ln_b, s0, *, nseq, seqlen, chunk, hpb, y_dtype):
    nc = seqlen // chunk
    npair = RW_HEADS // 2
    width = hpb * LANES
    row = lambda b, h, t: (b * nc + t, h)
    par = lambda b, h, t: (0, h)
    st = lambda b, h, t: (b, h, 0, 0)
    kern = functools.partial(_scan_kernel, chunk=chunk, hpb=hpb)
    return pl.pallas_call(
        kern,
        grid=(nseq, npair // hpb, nc),
        in_specs=[pl.BlockSpec((chunk, width), row)] * 7 + [pl.BlockSpec((1, width), par)] * 3
                 + [pl.BlockSpec((1, hpb, LANES, LANES), st)],
        out_specs=[pl.BlockSpec((chunk, width), row), pl.BlockSpec((1, hpb, LANES, LANES), st)],
        out_shape=[jax.ShapeDtypeStruct((nseq * seqlen, RW_WIDTH), y_dtype),
                   jax.ShapeDtypeStruct((nseq, npair, LANES, LANES), F32)],
        scratch_shapes=[pltpu.VMEM((hpb, LANES, LANES), F32)],
        compiler_params=_cparams(("parallel", "parallel", "arbitrary"), 32),
    )(r, lw, k, v, kk, a, g, r_k, ln_w, ln_b, s0)


def _pair_states(s):
    n = s.shape[0]
    s = s.reshape(n, RW_HEADS // 2, 2, RW_HEAD, RW_HEAD)
    z = jnp.zeros_like(s[:, :, 0])
    top = jnp.concatenate([s[:, :, 0], z], axis=-1)
    bot = jnp.concatenate([z, s[:, :, 1]], axis=-1)
    return jnp.concatenate([top, bot], axis=-2)


def _unpair_states(sp):
    n = sp.shape[0]
    s0 = sp[:, :, :RW_HEAD, :RW_HEAD]
    s1 = sp[:, :, RW_HEAD:, RW_HEAD:]
    return jnp.stack([s0, s1], axis=2).reshape(n, RW_HEADS, RW_HEAD, RW_HEAD)


def _kidx_kernel(x_ref, g_ref, kn_ref, kd_ref):
    x = x_ref[...]
    lane = lax.broadcasted_iota(I32, x.shape, 1)
    xk = jnp.where(lane < IDX_DIM, x, 0.0)
    ms = jnp.sum(xk * xk, axis=-1, keepdims=True) * (1.0 / IDX_DIM)
    xd = xk + pltpu.roll(xk, IDX_DIM, 1)
    kd = (xd * lax.rsqrt(ms + RMS_EPS)) * g_ref[...]
    kd_ref[...] = kd
    kn_ref[...] = kd[:, :IDX_DIM]


def kidx_norm(idx_cols, g, *, tm):
    m = idx_cols.shape[0]
    g2 = jnp.concatenate([g, g]).reshape(1, LANES)
    return pl.pallas_call(
        _kidx_kernel,
        grid=(m // tm,),
        in_specs=[pl.BlockSpec((tm, LANES), lambda i: (i, IDX_HEADS * IDX_DIM // LANES)),
                  pl.BlockSpec((1, LANES), lambda i: (0, 0))],
        out_specs=[pl.BlockSpec((tm, IDX_DIM), lambda i: (i, 0)), pl.BlockSpec((tm, LANES), lambda i: (i, 0))],
        out_shape=[jax.ShapeDtypeStruct((m, IDX_DIM), F32), jax.ShapeDtypeStruct((m, LANES), F32)],
        compiler_params=_cparams(("parallel",), 16),
    )(idx_cols, g2)


def _sort_key(score):
    score = jnp.where(score == 0.0, 0.0, score)
    bits = pltpu.bitcast(score, I32)
    return bits ^ (lax.shift_right_arithmetic(bits, 31) & 0x7FFFFFFF)


def _count_lanes(acc):
    return jnp.dot(acc.astype(BF16), jnp.ones((LANES, LANES), BF16), preferred_element_type=F32)


def _topk_select(key_tile, cut_ref, *, rows, n_tiles, topk, col_bits):
    def count(pred):
        acc = jnp.zeros((rows, LANES), F32)
        for t in range(n_tiles):
            acc = acc + jnp.where(pred(key_tile(t), t), 1.0, 0.0)
        return _count_lanes(acc)

    def bit_step(it, thr):
        cand = thr + lax.shift_left(jnp.int32(1), 31 - it)
        cnt = count(lambda kt, t: kt >= cand)
        return jnp.where(cnt >= topk, cand, thr)

    thr = lax.fori_loop(0, 32, bit_step, jnp.full((rows, LANES), INT_MIN, I32))
    n_gt = count(lambda kt, t: kt > thr)
    n_ge = count(lambda kt, t: kt >= thr)
    excess = (n_ge > topk) & (thr > NEG_INF_KEY)
    cut_ref[...] = jnp.full((rows, LANES), BIG_COL, I32)
    lane = lax.broadcasted_iota(I32, (rows, LANES), 1)

    @pl.when(jnp.max(jnp.where(excess, 1.0, 0.0)) > 0.0)
    def _():
        need = topk - n_gt

        def col_step(it, cut):
            cand = cut + lax.shift_left(jnp.int32(1), col_bits - 1 - it)
            cnt = count(lambda kt, t: (kt == thr) & (lane + t * LANES < cand))
            return jnp.where(cnt < need, cand, cut)

        cut = lax.fori_loop(0, col_bits, col_step, jnp.zeros((rows, LANES), I32))
        cut_ref[...] = jnp.where(excess, cut, BIG_COL)

    return thr


def _selected(kt, t, thr, cut):
    lane = lax.broadcasted_iota(I32, kt.shape, 1)
    return ((kt > thr) | ((kt == thr) & (lane + t * LANES <= cut))) & (kt != NEG_INF_KEY)


def _idx_scores(q_ref, wq_ref, kd, *, rows):
    lane = lax.broadcasted_iota(I32, (rows, LANES), 1)
    lo = lane < IDX_DIM
    score = jnp.zeros((rows, kd.shape[0]), F32)
    for hp in range(IDX_HEADS // 2):
        qp = q_ref[:, hp * LANES:(hp + 1) * LANES]
        for half in range(2):
            h = 2 * hp + half
            qh = jnp.where(lo, qp, 0.0) if half == 0 else jnp.where(lo, 0.0, qp)
            d = _dot_nt(qh, kd, precision=HIGHEST)
            wh = wq_ref[:, IDX_DIM + h:IDX_DIM + h + 1] * IDX_SCALE
            score = score + wh * jnp.maximum(d, 0.0)
    return score


def _prompt_index_kernel(q_ref, wq_ref, kd_ref, m_ref, key_scr, cut_scr, *, tq, seqlen, n_tiles, kblk, topk):
    i = pl.program_id(1)
    q_pos = i * tq + lax.broadcasted_iota(I32, (tq, kblk), 0)
    for c0 in range(0, n_tiles * LANES, kblk):
        score = _idx_scores(q_ref, wq_ref, kd_ref[0, c0:c0 + kblk, :], rows=tq)
        col = c0 + lax.broadcasted_iota(I32, (tq, kblk), 1)
        score = jnp.where(col <= q_pos, score, -jnp.inf)
        key_scr[:, c0:c0 + kblk] = _sort_key(score)
    key_tile = lambda t: key_scr[:, t * LANES:(t + 1) * LANES]
    thr = _topk_select(key_tile, cut_scr, rows=tq, n_tiles=n_tiles, topk=topk, col_bits=12)
    cut = cut_scr[...]
    for t in range(-(-seqlen // LANES)):
        width = min(LANES, seqlen - t * LANES)
        sel = jnp.where(_selected(key_tile(t), t, thr, cut), 1.0, 0.0).astype(m_ref.dtype)
        m_ref[:, t * LANES:t * LANES + width] = sel[:, :width]


def prompt_index_mask(idx_cols, kdup_pad, *, nseq, seqlen, tq, topk):
    nq = seqlen // tq
    lp = kdup_pad.shape[1]
    n_tiles = lp // LANES
    kblk = LANES
    for cand in (1024, 768, 512, 384, 256):
        if lp % cand == 0:
            kblk = cand
            break
    kern = functools.partial(_prompt_index_kernel, tq=tq, seqlen=seqlen, n_tiles=n_tiles, kblk=kblk, topk=topk)
    return pl.pallas_call(
        kern,
        grid=(nseq, nq),
        in_specs=[pl.BlockSpec((tq, IDX_HEADS * IDX_DIM), lambda b, i: (b * nq + i, 0)),
                  pl.BlockSpec((tq, LANES), lambda b, i: (b * nq + i, IDX_HEADS * IDX_DIM // LANES)),
                  pl.BlockSpec((1, lp, LANES), lambda b, i: (b, 0, 0))],
        out_specs=pl.BlockSpec((tq, seqlen), lambda b, i: (b * nq + i, 0)),
        out_shape=jax.ShapeDtypeStruct((nseq * seqlen, seqlen), BF16),
        scratch_shapes=[pltpu.VMEM((tq, lp), I32), pltpu.VMEM((tq, LANES), I32)],
        compiler_params=_cparams(("parallel", "arbitrary")),
    )(idx_cols, idx_cols, kdup_pad)


def _prompt_attn_kernel(q_ref, k_ref, v_ref, m_ref, o_ref):
    logits = _dot_nt(q_ref[...], k_ref[...]) * ATT_SCALE
    logits = jnp.where(m_ref[...] > 0, logits, -jnp.inf)
    mx = jnp.max(logits, axis=-1, keepdims=True)
    e = jnp.exp(logits - mx)
    prob = e * (1.0 / jnp.sum(e, axis=-1, keepdims=True))
    o_ref[...] = _dot(prob, v_ref[...]).astype(o_ref.dtype)


def prompt_attention(qkv, mask, *, nseq, seqlen, tq):
    nq = seqlen // tq
    return pl.pallas_call(
        _prompt_attn_kernel,
        grid=(nseq, ATT_HEADS, nq),
        in_specs=[pl.BlockSpec((tq, ATT_HEAD), lambda b, h, i: (b * nq + i, h)),
                  pl.BlockSpec((seqlen, ATT_HEAD), lambda b, h, i: (b, ATT_HEADS + h)),
                  pl.BlockSpec((seqlen, ATT_HEAD), lambda b, h, i: (b, 2 * ATT_HEADS + h)),
                  pl.BlockSpec((tq, seqlen), lambda b, h, i: (b * nq + i, 0))],
        out_specs=pl.BlockSpec((tq, ATT_HEAD), lambda b, h, i: (b * nq + i, h)),
        out_shape=jax.ShapeDtypeStruct((nseq * seqlen, ATT_WIDTH), BF16),
        compiler_params=_cparams(("parallel", "parallel", "arbitrary")),
    )(qkv, qkv, qkv, mask)


def _sample_score_kernel(pt_ref, q_ref, w_ref, page_ref, o_ref, *, dsq):
    page = page_ref[0, 0]
    d = _dot_nt(q_ref[0], page, precision=HIGHEST)
    d = jnp.maximum(d, 0.0) * w_ref[0]
    o_ref[0] = jnp.sum(d.reshape(IDX_HEADS, dsq, PAGE_SIZE), axis=0)


def sample_page_scores(page_table, qh, wh, cache_kidx, layer, *, nb, dsq, n_pages):
    kern = functools.partial(_sample_score_kernel, dsq=dsq)
    rows = IDX_HEADS * dsq
    grid_spec = pltpu.PrefetchScalarGridSpec(
        num_scalar_prefetch=1,
        grid=(nb, n_pages),
        in_specs=[pl.BlockSpec((1, rows, IDX_DIM), lambda b, p, pt: (b, 0, 0)),
                  pl.BlockSpec((1, rows, 1), lambda b, p, pt: (b, 0, 0)),
                  pl.BlockSpec((1, 1, PAGE_SIZE, IDX_DIM), lambda b, p, pt: (layer, pt[b * n_pages + p], 0, 0))],
        out_specs=pl.BlockSpec((1, dsq, PAGE_SIZE), lambda b, p, pt: (b, 0, p)),
    )
    return pl.pallas_call(
        kern,
        grid_spec=grid_spec,
        out_shape=jax.ShapeDtypeStruct((nb, dsq, n_pages * PAGE_SIZE), F32),
        compiler_params=_cparams(("parallel", "arbitrary"), 16),
    )(page_table.reshape(-1), qh, wh, cache_kidx)


def _sample_select_kernel(sc_ref, q_ref, w_ref, kn_ref, idx_ref, key_scr, cut_scr, *, dsq, past, topk):
    n_past_tiles = past // LANES
    n_tiles = n_past_tiles + 1
    for t in range(n_past_tiles):
        key_scr[t] = _sort_key(sc_ref[0, :, t * LANES:(t + 1) * LANES])
    d = _dot_nt(q_ref[0], kn_ref[0], precision=HIGHEST)
    d = jnp.maximum(d, 0.0) * w_ref[0]
    s_new = jnp.sum(d.reshape(IDX_HEADS, dsq, LANES), axis=0)
    qrow = lax.broadcasted_iota(I32, (dsq, LANES), 0)
    jcol = lax.broadcasted_iota(I32, (dsq, LANES), 1)
    s_new = jnp.where(jcol <= qrow, s_new, -jnp.inf)
    key_scr[n_past_tiles] = _sort_key(s_new)
    thr = _topk_select(lambda t: key_scr[t], cut_scr, rows=dsq, n_tiles=n_tiles, topk=topk, col_bits=15)
    cut = cut_scr[...]

    upper = jnp.where(lax.broadcasted_iota(I32, (LANES, LANES), 0) < lax.broadcasted_iota(I32, (LANES, LANES), 1),
                      1.0, 0.0).astype(BF16)
    slot = lax.broadcasted_iota(I32, (topk, LANES), 0)
    lane_f = lax.broadcasted_iota(I32, (topk, LANES), 1).astype(F32)
    sub = 16
    for q in range(dsq):
        thr_q = thr[q:q + 1]
        cut_q = cut[q:q + 1]

        def tile_step(t, carry):
            offset, acc = carry
            kt = key_scr[t][q:q + 1]
            colv = lax.broadcasted_iota(I32, (1, LANES), 1) + t * LANES
            sel = ((kt > thr_q) | ((kt == thr_q) & (colv <= cut_q))) & (kt != NEG_INF_KEY)
            self_ = jnp.where(sel, 1.0, 0.0)
            selb = jnp.broadcast_to(self_, (sub, LANES)).astype(BF16)
            rank = offset + jnp.dot(selb, upper, preferred_element_type=F32)[0:1]
            total = _count_lanes(jnp.broadcast_to(self_, (sub, LANES)))[0:1]
            hit = (jnp.broadcast_to(self_, (topk, LANES)) > 0.0) & (
                jnp.broadcast_to(rank, (topk, LANES)).astype(I32) == slot)
            acc = acc + jnp.where(hit, lane_f + jnp.asarray(t * LANES).astype(F32), 0.0)
            return offset + total, acc

        _, acc = lax.fori_loop(0, n_tiles, tile_step,
                               (jnp.zeros((1, LANES), F32), jnp.zeros((topk, LANES), F32)))
        idx_ref[0, q] = jnp.sum(acc, axis=1, keepdims=True).astype(I32)


def sample_select(scores, qh, wh, kn_pad, *, nb, dsq, past, topk):
    rows = IDX_HEADS * dsq
    n_tiles = past // LANES + 1
    kern = functools.partial(_sample_select_kernel, dsq=dsq, past=past, topk=topk)
    return pl.pallas_call(
        kern,
        grid=(nb,),
        in_specs=[pl.BlockSpec((1, dsq, past), lambda b: (b, 0, 0)),
                  pl.BlockSpec((1, rows, IDX_DIM), lambda b: (b, 0, 0)),
                  pl.BlockSpec((1, rows, 1), lambda b: (b, 0, 0)),
                  pl.BlockSpec((1, LANES, IDX_DIM), lambda b: (b, 0, 0))],
        out_specs=pl.BlockSpec((1, dsq, topk, 1), lambda b: (b, 0, 0, 0)),
        out_shape=jax.ShapeDtypeStruct((nb, dsq, topk, 1), I32),
        scratch_shapes=[pltpu.VMEM((n_tiles, dsq, LANES), I32), pltpu.VMEM((dsq, LANES), I32)],
        compiler_params=_cparams(("parallel",), 32),
    )(scores, qh, wh, kn_pad)


def _sample_attn_kernel(idx_ref, pt_ref, q_ref, kn_ref, vn_ref, ck_ref, cv_ref, o_ref,
                        kbuf, vbuf, sem, *, layer, dsq, past, n_pages, topk):
    b = pl.program_id(0)

    def row_copies(q, j, slot):
        pos = idx_ref[(b * dsq + q) * topk + j]
        in_past = pos < past
        ppos = jnp.minimum(pos, past - 1)
        page = pt_ref[b * n_pages + ppos // PAGE_SIZE]
        off = ppos % PAGE_SIZE
        npos = jnp.clip(pos - past, 0, dsq - 1)
        return in_past, (
            pltpu.make_async_copy(ck_ref.at[layer, page, off], kbuf.at[slot, :, j], sem.at[0, slot]),
            pltpu.make_async_copy(cv_ref.at[layer, page, off], vbuf.at[slot, :, j], sem.at[1, slot]),
            pltpu.make_async_copy(kn_ref.at[b, npos], kbuf.at[slot, :, j], sem.at[0, slot]),
            pltpu.make_async_copy(vn_ref.at[b, npos], vbuf.at[slot, :, j], sem.at[1, slot]))

    def start_query(q, slot):
        def body(j, carry):
            in_past, (ck, cv, nk, nv) = row_copies(q, j, slot)

            @pl.when(in_past)
            def _():
                ck.start()
                cv.start()

            @pl.when(jnp.logical_not(in_past))
            def _():
                nk.start()
                nv.start()

            return carry

        lax.fori_loop(0, topk, body, 0)

    def wait_query(slot):
        def body(j, carry):
            pltpu.make_async_copy(kn_ref.at[0, 0], kbuf.at[slot, :, j], sem.at[0, slot]).wait()
            pltpu.make_async_copy(vn_ref.at[0, 0], vbuf.at[slot, :, j], sem.at[1, slot]).wait()
            return carry

        lax.fori_loop(0, topk, body, 0)

    start_query(0, 0)
    for q in range(dsq):
        slot = q % 2
        if q + 1 < dsq:
            start_query(q + 1, 1 - slot)
        wait_query(slot)
        outs = []
        for h in range(ATT_HEADS):
            qh = jnp.broadcast_to(q_ref[0, q, h:h + 1, :], (8, ATT_HEAD))
            logits = _dot_nt(qh, kbuf[slot, h])[0:1] * ATT_SCALE
            mx = jnp.max(logits, axis=-1, keepdims=True)
            e = jnp.exp(logits - mx)
            prob = e / jnp.sum(e, axis=-1, keepdims=True)
            outs.append(_dot(jnp.broadcast_to(prob, (8, topk)), vbuf[slot, h])[0:1])
        o_ref[0, q:q + 1, :] = jnp.concatenate(outs, axis=1)


def sample_attention(sel_idx, page_table, q, k_new, v_new, cache_k, cache_v, layer, *, nb, dsq, past, topk):
    n_pages = past // PAGE_SIZE
    kern = functools.partial(_sample_attn_kernel, layer=layer, dsq=dsq, past=past, n_pages=n_pages, topk=topk)
    grid_spec = pltpu.PrefetchScalarGridSpec(
        num_scalar_prefetch=2,
        grid=(nb,),
        in_specs=[pl.BlockSpec((1, dsq, ATT_HEADS, ATT_HEAD), lambda b, idx, pt: (b, 0, 0, 0)),
                  pl.BlockSpec(memory_space=pl.ANY), pl.BlockSpec(memory_space=pl.ANY),
                  pl.BlockSpec(memory_space=pl.ANY), pl.BlockSpec(memory_space=pl.ANY)],
        out_specs=pl.BlockSpec((1, dsq, ATT_WIDTH), lambda b, idx, pt: (b, 0, 0)),
        scratch_shapes=[pltpu.VMEM((2, ATT_HEADS, topk, ATT_HEAD), F32),
                        pltpu.VMEM((2, ATT_HEADS, topk, ATT_HEAD), F32),
                        pltpu.SemaphoreType.DMA((2, 2))],
    )
    return pl.pallas_call(
        kern,
        grid_spec=grid_spec,
        out_shape=jax.ShapeDtypeStruct((nb, dsq, ATT_WIDTH), F32),
        compiler_params=_cparams(("arbitrary",), 32),
    )(sel_idx.reshape(-1), page_table.reshape(-1), q, k_new, v_new, cache_k, cache_v)


def _conv_gate_kernel(ug_ref, uv_ref, pg_ref, pv_ref, wg_ref, wv_ref, bg_ref, bv_ref, o_ref, *, nb, seqlen):
    def conv(u, prev, w, bias):
        return w[0:1, :] * _shift_rows(u, prev, 2) + w[1:2, :] * _shift_rows(u, prev[1:2, :], 1) + w[2:3, :] * u + bias

    outs = []
    for s in range(nb):
        rows = slice(s * seqlen, (s + 1) * seqlen)
        gate = conv(ug_ref[rows, :], pg_ref[s], wg_ref[...], bg_ref[...])
        val = conv(uv_ref[rows, :], pv_ref[s], wv_ref[...], bv_ref[...])
        outs.append(gate * jax.nn.sigmoid(gate) * val)
    o_ref[...] = (outs[0] if nb == 1 else jnp.concatenate(outs, axis=0)).astype(o_ref.dtype)


def conv_gate(u, conv_prev, conv_w, conv_b, *, nseq, seqlen, nb, tc):
    m = nseq * seqlen
    half = D_FF // tc
    kern = functools.partial(_conv_gate_kernel, nb=nb, seqlen=seqlen)
    rows = nb * seqlen
    return pl.pallas_call(
        kern,
        grid=(nseq // nb, half),
        in_specs=[pl.BlockSpec((rows, tc), lambda b, j: (b, j)),
                  pl.BlockSpec((rows, tc), lambda b, j: (b, j + half)),
                  pl.BlockSpec((nb, CONV_W - 1, tc), lambda b, j: (b, 0, j)),
                  pl.BlockSpec((nb, CONV_W - 1, tc), lambda b, j: (b, 0, j + half)),
                  pl.BlockSpec((CONV_W, tc), lambda b, j: (0, j)),
                  pl.BlockSpec((CONV_W, tc), lambda b, j: (0, j + half)),
                  pl.BlockSpec((1, tc), lambda b, j: (0, j)),
                  pl.BlockSpec((1, tc), lambda b, j: (0, j + half))],
        out_specs=pl.BlockSpec((rows, tc), lambda b, j: (b, j)),
        out_shape=jax.ShapeDtypeStruct((m, D_FF), BF16),
        compiler_params=_cparams(("parallel", "arbitrary"), 48),
    )(u, u, conv_prev, conv_prev, conv_w, conv_w, conv_b, conv_b)


def _layer_weights(l, w_in, rw_mu, rw_g2):
    wl = w_in[l]
    a0 = RW_COLS
    w_rw = jnp.pad(wl[:, :a0], ((0, 0), (0, RW_PAD - RW_COLS))).astype(BF16)
    w_qkv = wl[:, a0:a0 + 3 * ATT_WIDTH].astype(BF16)
    i0 = a0 + 3 * ATT_WIDTH
    w_idx = jnp.pad(wl[:, i0:a0 + ATT_COLS], ((0, 0), (0, IDX_PAD - (ATT_COLS - 3 * ATT_WIDTH)))).astype(BF16)
    w_gate = wl[:, a0 + ATT_COLS:].astype(BF16)
    mu = jnp.pad(rw_mu[l], (0, RW_PAD - RW_COLS)).reshape(1, RW_PAD)
    g2 = jnp.pad(rw_g2[l], ((0, G_PAD - G_LORA), (0, 0)))
    return w_rw, w_qkv, w_idx, w_gate, mu, g2


def _trunk_layer(x, l, p, lw, shift_prev, wkv_prev, conv_prev, attend, *, nseq, seqlen, tiles):
    m = nseq * seqlen
    w_rw, w_qkv, w_idx, w_gate, mu, g2 = lw
    row = lambda a: a.reshape(1, -1)
    h = rmsnorm_rows(x, p['norm_mix'][l], tm=tiles['rms'], out_dtype=BF16)
    rw = matmul(h, w_rw, tm=tiles['tm'], tn=tiles['tn_rw'])
    qkv = matmul(h, w_qkv, tm=tiles['tm'], tn=tiles['tn_qkv'])
    idx_cols = matmul(h, w_idx, tm=tiles['tm'], tn=tiles['tn_idx'])
    gates = matmul(h, w_gate, tm=tiles['tm'], tn=tiles['tn_gate'])

    sp = jnp.pad(shift_prev, ((0, 0), (0, 0), (0, RW_PAD - RW_COLS)))
    r, lwd, k2, v, kk, a, g = rw_prep(rw, sp, mu, row(p['rw_w0'][l]), p['rw_w2'][l], row(p['rw_a0'][l]),
                                       p['rw_a2'][l], g2, row(p['rw_k_k'][l]), row(p['rw_k_a'][l]),
                                       nseq=nseq, seqlen=seqlen, tt=tiles['chunk'])
    y_a, s_pair = rw_scan(r, lwd, k2, v, kk, a, g, row(p['rw_r_k'][l]), row(p['rw_ln_w'][l]), row(p['rw_ln_b'][l]),
                          _pair_states(wkv_prev), nseq=nseq, seqlen=seqlen, chunk=tiles['chunk'], hpb=tiles['hpb'],
                          y_dtype=tiles['y_dtype'])
    y_a = y_a.astype(BF16)
    wkv_new = _unpair_states(s_pair)
    shift_new = rw.reshape(nseq, seqlen, RW_PAD)[:, seqlen - 1:, :RW_COLS]

    ki_n, ki_dup = kidx_norm(idx_cols, p['idx_k_norm'][l], tm=tiles['kidx'])
    y_b = attend(qkv, idx_cols, ki_n, ki_dup)

    mix = branch_mix(y_a, y_b, p['w_branch_a'][l], p['w_branch_b'][l], gates, tm=tiles['tm'], tn=tiles['tn_mix'])
    x = matmul(mix, p['w_out'][l], tm=tiles['tm'], tn=tiles['tn_out'], residual=x)

    h2 = rmsnorm_rows(x, p['norm_ffn'][l], tm=tiles['rms'], out_dtype=BF16)
    u = matmul(h2, p['w_up'][l], tm=tiles['tm'], tn=tiles['tn_up'])
    act = conv_gate(u, conv_prev, p['conv_w'][l], row(p['conv_b'][l]), nseq=nseq, seqlen=seqlen,
                    nb=tiles['conv_nb'], tc=tiles['conv_tc'])
    u3 = u.reshape(nseq, seqlen, 2 * D_FF)
    conv_new = jnp.concatenate([conv_prev, u3], axis=1)[:, seqlen:] if seqlen < CONV_W - 1 else u3[:, seqlen - (CONV_W - 1):]
    x = matmul(act, p['w_down_bf16'][l], tm=tiles['tm_down'], tn=tiles['tn_down'], residual=x)

    k_new = qkv[:, ATT_WIDTH:2 * ATT_WIDTH].reshape(nseq, seqlen, ATT_HEADS, ATT_HEAD)
    v_new = qkv[:, 2 * ATT_WIDTH:].reshape(nseq, seqlen, ATT_HEADS, ATT_HEAD)
    return x, (k_new, v_new, ki_n.reshape(nseq, seqlen, IDX_DIM), wkv_new, shift_new, conv_new)


PROMPT_TILES = dict(rms=688, tm=1376, tn_rw=768, tn_qkv=768, tn_idx=384, tn_gate=512, tn_mix=256, tn_out=256, tn_up=512,
                    tm_down=688, tn_down=256, chunk=48, hpb=4, kidx=688, conv_nb=1, conv_tc=256, tq=688, y_dtype=BF16)
SAMPLE_TILES = dict(rms=64, tm=64, tn_rw=1152, tn_qkv=1024, tn_idx=1152, tn_gate=1024, tn_mix=1024, tn_out=1024, tn_up=512,
                    tm_down=64, tn_down=512, chunk=8, hpb=4, kidx=64, conv_nb=8, conv_tc=256, y_dtype=F32)


def kernel(x_prompt, x_sample, cache_k, cache_v, cache_kidx, state_wkv, state_shift, state_conv, page_table,
           meta_tokens, norm_mix, w_in, rw_mu, rw_w0, rw_w2, rw_a0, rw_a2, rw_g2, rw_k_k, rw_k_a, rw_r_k,
           rw_ln_w, rw_ln_b, idx_k_norm, w_branch_a, w_branch_b, w_out, norm_ffn, w_up, conv_w, conv_b,
           w_down, norm_final):
    p = {'norm_mix': norm_mix, 'rw_w0': rw_w0, 'rw_w2': rw_w2, 'rw_a0': rw_a0, 'rw_a2': rw_a2,
         'rw_k_k': rw_k_k, 'rw_k_a': rw_k_a, 'rw_r_k': rw_r_k.reshape(rw_r_k.shape[0], RW_WIDTH),
         'rw_ln_w': rw_ln_w, 'rw_ln_b': rw_ln_b, 'idx_k_norm': idx_k_norm, 'w_branch_a': w_branch_a,
         'w_branch_b': w_branch_b, 'w_out': w_out, 'norm_ffn': norm_ffn, 'w_up': w_up, 'conv_w': conv_w,
         'conv_b': conv_b, 'w_down_bf16': w_down.astype(BF16)}
    depth = w_in.shape[0]
    nb_p, seq = x_prompt.shape[:2]
    tp = seq + N_META
    nb_s, ds = x_sample.shape[:2]
    n_pages = page_table.shape[1]
    past = n_pages * PAGE_SIZE

    meta = jnp.broadcast_to(meta_tokens[None], (nb_p, N_META, D_MODEL))
    xp = jnp.concatenate([meta, x_prompt], axis=1).reshape(nb_p * tp, D_MODEL)
    xs = x_sample.reshape(nb_s * ds, D_MODEL)

    topk_p = min(TOPK_MAX, tp // 4)
    topk_s = min(TOPK_MAX, (past + ds) // 4)
    lp = -(-tp // (2 * LANES)) * 2 * LANES
    tq = PROMPT_TILES['tq']

    def prompt_attend(qkv, idx_cols, ki_n, ki_dup):
        kd = jnp.pad(ki_dup.reshape(nb_p, tp, LANES), ((0, 0), (0, lp - tp), (0, 0)))
        mask = prompt_index_mask(idx_cols, kd, nseq=nb_p, seqlen=tp, tq=tq, topk=topk_p)
        return prompt_attention(qkv, mask, nseq=nb_p, seqlen=tp, tq=tq)

    def make_sample_attend(l):
        def attend(qkv, idx_cols, ki_n, ki_dup):
            qi = idx_cols[:, :IDX_HEADS * IDX_DIM].reshape(nb_s, ds, IDX_HEADS, IDX_DIM)
            qh = jnp.swapaxes(qi, 1, 2).reshape(nb_s, IDX_HEADS * ds, IDX_DIM)
            wi = idx_cols[:, IDX_HEADS * IDX_DIM + IDX_DIM:IDX_HEADS * IDX_DIM + IDX_DIM + IDX_HEADS]
            wh = jnp.swapaxes(wi.reshape(nb_s, ds, IDX_HEADS), 1, 2).reshape(nb_s, IDX_HEADS * ds, 1) * IDX_SCALE
            scores = sample_page_scores(page_table, qh, wh, cache_kidx, l, nb=nb_s, dsq=ds, n_pages=n_pages)
            kn_pad = jnp.pad(ki_n.reshape(nb_s, ds, IDX_DIM), ((0, 0), (0, LANES - ds), (0, 0)))
            sel = sample_select(scores, qh, wh, kn_pad, nb=nb_s, dsq=ds, past=past, topk=topk_s)
            q = qkv[:, :ATT_WIDTH].reshape(nb_s, ds, ATT_HEADS, ATT_HEAD)
            k_new = qkv[:, ATT_WIDTH:2 * ATT_WIDTH].reshape(nb_s, ds, ATT_HEADS, ATT_HEAD)
            v_new = qkv[:, 2 * ATT_WIDTH:].reshape(nb_s, ds, ATT_HEADS, ATT_HEAD)
            y = sample_attention(sel, page_table, q, k_new, v_new, cache_k, cache_v, l,
                                 nb=nb_s, dsq=ds, past=past, topk=topk_s)
            return y.reshape(nb_s * ds, ATT_WIDTH).astype(BF16)
        return attend

    zero_shift = jnp.zeros((nb_p, 1, RW_COLS), F32)
    zero_wkv = jnp.zeros((nb_p, RW_HEADS, RW_HEAD, RW_HEAD), F32)
    zero_conv = jnp.zeros((nb_p, CONV_W - 1, 2 * D_FF), F32)
    sts_p, sts_s = [], []
    for l in range(depth):
        lw = _layer_weights(l, w_in, rw_mu, rw_g2)
        xp, st = _trunk_layer(xp, l, p, lw, zero_shift, zero_wkv, zero_conv, prompt_attend,
                              nseq=nb_p, seqlen=tp, tiles=PROMPT_TILES)
        sts_p.append(st)
        xs, st = _trunk_layer(xs, l, p, lw, state_shift[l], state_wkv[l], state_conv[l], make_sample_attend(l),
                              nseq=nb_s, seqlen=ds, tiles=SAMPLE_TILES)
        sts_s.append(st)

    y_prompt = rmsnorm_rows(xp, norm_final, tm=PROMPT_TILES['rms'], out_dtype=F32)
    y_prompt = y_prompt.reshape(nb_p, tp, D_MODEL)[:, N_META:]
    y_sample = rmsnorm_rows(xs, norm_final, tm=SAMPLE_TILES['rms'], out_dtype=F32).reshape(nb_s, ds, D_MODEL)
    stk = lambda sts, i: jnp.stack([s[i] for s in sts])
    return (y_prompt, y_sample,
            stk(sts_p, 0), stk(sts_p, 1), stk(sts_p, 2), stk(sts_p, 3), stk(sts_p, 4), stk(sts_p, 5),
            stk(sts_s, 0), stk(sts_s, 1), stk(sts_s, 2), stk(sts_s, 3), stk(sts_s, 4), stk(sts_s, 5))
```

```python
import functools

import jax
import jax.numpy as jnp
from jax import lax
from jax.experimental import pallas as pl
from jax.experimental.pallas import tpu as pltpu

F32 = jnp.float32
BF16 = jnp.bfloat16
I32 = jnp.int32
HIGHEST = lax.Precision.HIGHEST

D_MODEL = 4096
N_META = 16
PAGE_SIZE = 128
RW_HEAD = 64
RW_HEADS = 32
RW_WIDTH = RW_HEADS * RW_HEAD
W_LORA = 128
A_LORA = 128
G_LORA = 480
RW_COLS = 3 * RW_WIDTH + W_LORA + A_LORA + G_LORA
ATT_HEADS = 16
ATT_HEAD = 128
ATT_WIDTH = ATT_HEADS * ATT_HEAD
IDX_HEADS = 16
IDX_DIM = 64
TOPK_MAX = 256
ATT_COLS = 3 * ATT_WIDTH + IDX_HEADS * IDX_DIM + IDX_DIM + IDX_HEADS
D_FF = 11008
CONV_W = 3
RMS_EPS = 1e-6
GN_EPS = 64e-5
ATT_SCALE = ATT_HEAD ** -0.5
IDX_SCALE = (IDX_HEADS ** -0.5) * (IDX_DIM ** -0.5)

LANES = 128
RW_PAD = 6912
IDX_PAD = 1152
G_PAD = 512
INT_MIN = -2 ** 31
NEG_INF_KEY = INT_MIN + 0x7FFFFF
BIG_COL = 2 ** 30
VMEM_MB = 60


def _cparams(sem, mb=VMEM_MB):
    return pltpu.CompilerParams(dimension_semantics=sem, vmem_limit_bytes=mb * 1024 * 1024)


def _dot(a, b):
    return jnp.dot(a.astype(BF16), b.astype(BF16), preferred_element_type=F32)


def _dot_nt(a, b, precision=None):
    if precision is None:
        a, b = a.astype(BF16), b.astype(BF16)
    return lax.dot_general(a, b, (((1,), (1,)), ((), ())), precision=precision,
                           preferred_element_type=F32)


def _dot_hi(a, b):
    return jnp.dot(a, b, precision=HIGHEST, preferred_element_type=F32)


def _split_bf16(x):
    hi = x.astype(BF16)
    return hi, (x - hi.astype(F32)).astype(BF16)


def _dot_split(x, m_bf16):
    hi, lo = _split_bf16(x)
    return (jnp.dot(hi, m_bf16, preferred_element_type=F32) + jnp.dot(lo, m_bf16, preferred_element_type=F32))


def _dot_split_rhs(m_bf16, x):
    hi, lo = _split_bf16(x)
    return (jnp.dot(m_bf16, hi, preferred_element_type=F32) + jnp.dot(m_bf16, lo, preferred_element_type=F32))


def _rms_kernel(x_ref, g_ref, o_ref):
    x = x_ref[...]
    ms = jnp.sum(x * x, axis=-1, keepdims=True) * (1.0 / x.shape[-1])
    o_ref[...] = ((x * lax.rsqrt(ms + RMS_EPS)) * g_ref[...]).astype(o_ref.dtype)


def rmsnorm_rows(x, g, *, tm, out_dtype):
    m, d = x.shape
    return pl.pallas_call(
        _rms_kernel,
        grid=(m // tm,),
        in_specs=[pl.BlockSpec((tm, d), lambda i: (i, 0)), pl.BlockSpec((1, d), lambda i: (0, 0))],
        out_specs=pl.BlockSpec((tm, d), lambda i: (i, 0)),
        out_shape=jax.ShapeDtypeStruct((m, d), out_dtype),
        compiler_params=_cparams(("parallel",)),
    )(x, g.reshape(1, d))


def _mm_kernel(x_ref, w_ref, o_ref):
    o_ref[...] = _dot(x_ref[...], w_ref[...]).astype(o_ref.dtype)


def _mm_res_kernel(x_ref, w_ref, r_ref, o_ref):
    o_ref[...] = (r_ref[...] + _dot(x_ref[...], w_ref[...])).astype(o_ref.dtype)


def matmul(x, w, *, tm, tn, out_dtype=F32, residual=None):
    m, k = x.shape
    n = w.shape[1]
    assert m % tm == 0 and n % tn == 0, (m, tm, n, tn)
    in_specs = [pl.BlockSpec((tm, k), lambda i, j: (i, 0)), pl.BlockSpec((k, tn), lambda i, j: (0, j))]
    args = [x, w]
    kern = _mm_kernel
    if residual is not None:
        in_specs.append(pl.BlockSpec((tm, tn), lambda i, j: (i, j)))
        args.append(residual)
        kern = _mm_res_kernel
    return pl.pallas_call(
        kern,
        grid=(m // tm, n // tn),
        in_specs=in_specs,
        out_specs=pl.BlockSpec((tm, tn), lambda i, j: (i, j)),
        out_shape=jax.ShapeDtypeStruct((m, n), out_dtype),
        compiler_params=_cparams(("parallel", "arbitrary")),
    )(*args)


def _mix_kernel(ya_ref, yb_ref, wa_ref, wb_ref, ga_ref, gb_ref, o_ref):
    pa = _dot(ya_ref[...], wa_ref[...])
    pb = _dot(yb_ref[...], wb_ref[...])
    o_ref[...] = (jax.nn.sigmoid(ga_ref[...]) * pa + jax.nn.sigmoid(gb_ref[...]) * pb).astype(o_ref.dtype)


def branch_mix(ya, yb, wa, wb, gates, *, tm, tn):
    m, k = ya.shape
    n = wa.shape[1]
    assert m % tm == 0 and n % tn == 0, (m, tm, n, tn)
    nb = n // tn
    return pl.pallas_call(
        _mix_kernel,
        grid=(m // tm, nb),
        in_specs=[pl.BlockSpec((tm, k), lambda i, j: (i, 0)),
                  pl.BlockSpec((tm, k), lambda i, j: (i, 0)),
                  pl.BlockSpec((k, tn), lambda i, j: (0, j)),
                  pl.BlockSpec((k, tn), lambda i, j: (0, j)),
                  pl.BlockSpec((tm, tn), lambda i, j: (i, j)),
                  pl.BlockSpec((tm, tn), lambda i, j: (i, j + nb))],
        out_specs=pl.BlockSpec((tm, tn), lambda i, j: (i, j)),
        out_shape=jax.ShapeDtypeStruct((m, n), BF16),
        compiler_params=_cparams(("parallel", "arbitrary")),
    )(ya, yb, wa, wb, gates, gates)


def _shift_rows(x, first_rows, n):
    rolled = pltpu.roll(x, n, 0)
    row = lax.broadcasted_iota(I32, x.shape, 0)
    out = rolled
    for j in range(n):
        out = jnp.where(row == j, first_rows[j:j + 1, :], out)
    return out


def _rw_prep_kernel(x_ref, sp_ref, mu_ref, w0_ref, w2_ref, a0_ref, a2_ref, g2_ref, kk_ref, ka_ref,
                    r_o, lw_o, k_o, v_o, kk_o, a_o, g_o, carry):
    t = pl.program_id(1)
    tt = x_ref.shape[0]

    @pl.when(t == 0)
    def _():
        carry[...] = sp_ref[0]

    x = x_ref[...]
    xprev = _shift_rows(x, carry[...], 1)
    carry[...] = x[tt - 1:tt, :]
    xs = x + (xprev - x) * mu_ref[...]
    w = RW_WIDTH
    r = xs[:, 0:w]
    k = xs[:, w:2 * w]
    v = xs[:, 2 * w:3 * w]
    wd = xs[:, 3 * w:3 * w + W_LORA]
    ad = xs[:, 3 * w + W_LORA:3 * w + W_LORA + A_LORA]
    gd = xs[:, 3 * w + W_LORA + A_LORA:RW_PAD]
    z = -(w0_ref[...] + _dot(jnp.tanh(wd), w2_ref[...]))
    softplus = jnp.maximum(z, 0.0) + jnp.log(1.0 + jnp.exp(-jnp.abs(z)))
    w_log = -softplus - 0.5
    a = jax.nn.sigmoid(a0_ref[...] + _dot(ad, a2_ref[...]))
    r_o[...] = r
    lw_o[...] = -jnp.exp(w_log)
    k_o[...] = k * (1.0 + (a - 1.0) * ka_ref[...])
    v_o[...] = v
    kk_o[...] = k * kk_ref[...]
    a_o[...] = a
    g_o[...] = _dot(jax.nn.sigmoid(gd), g2_ref[...])


def rw_prep(rw, shift_prev, mu, w0, w2, a0, a2, g2, k_k, k_a, *, nseq, seqlen, tt):
    nt = seqlen // tt
    row = lambda b, t: (b * nt + t, 0)
    const = lambda b, t: (0, 0)
    vec = lambda n: pl.BlockSpec((1, n), const)
    out_sds = jax.ShapeDtypeStruct((nseq * seqlen, RW_WIDTH), F32)
    return pl.pallas_call(
        _rw_prep_kernel,
        grid=(nseq, nt),
        in_specs=[pl.BlockSpec((tt, RW_PAD), row),
                  pl.BlockSpec((1, 1, RW_PAD), lambda b, t: (b, 0, 0)),
                  vec(RW_PAD), vec(RW_WIDTH),
                  pl.BlockSpec((W_LORA, RW_WIDTH), const), vec(RW_WIDTH),
                  pl.BlockSpec((A_LORA, RW_WIDTH), const),
                  pl.BlockSpec((G_PAD, RW_WIDTH), const), vec(RW_WIDTH), vec(RW_WIDTH)],
        out_specs=[pl.BlockSpec((tt, RW_WIDTH), row)] * 7,
        out_shape=[out_sds] * 7,
        scratch_shapes=[pltpu.VMEM((1, RW_PAD), F32)],
        compiler_params=_cparams(("arbitrary", "arbitrary"), 48),
    )(rw, shift_prev, mu, w0, w2, a0, a2, g2, k_k, k_a)


def _pad_rows(x, rows):
    if x.shape[0] == rows:
        return x
    return jnp.concatenate([x, jnp.zeros((rows - x.shape[0], x.shape[1]), x.dtype)], axis=0)


def _scan_kernel(r_ref, lw_ref, k_ref, v_ref, kk_ref, a_ref, g_ref, rk_ref, lnw_ref, lnb_ref, s0_ref,
                 y_ref, so_ref, s_scr, *, chunk, hpb):
    c = chunk
    c2 = 2 * c
    ci = pl.program_id(2)

    @pl.when(ci == 0)
    def _():
        s_scr[...] = s0_ref[0]

    lane = lax.broadcasted_iota(I32, (c, LANES), 1)
    m0 = lane < RW_HEAD
    ei = lax.broadcasted_iota(I32, (LANES, LANES), 0)
    ej = lax.broadcasted_iota(I32, (LANES, LANES), 1)
    same_head = lax.shift_right_logical(ei, 6) == lax.shift_right_logical(ej, 6)
    e_sum = jnp.where(same_head, 1.0, 0.0).astype(F32)
    e_mean = e_sum * (1.0 / RW_HEAD)
    ti = lax.broadcasted_iota(I32, (c, c), 0)
    tj = lax.broadcasted_iota(I32, (c, c), 1)
    tril_incl = jnp.where(ti >= tj, 1.0, 0.0).astype(F32)
    ri = lax.broadcasted_iota(I32, (c2, LANES), 0)
    cj = lax.broadcasted_iota(I32, (c2, LANES), 1)
    rl = jnp.where(ri >= c, ri - c, ri)
    cl = jnp.where(cj >= c, cj - c, cj)
    same_blk = (jnp.where(ri >= c, 1, 0) == jnp.where(cj >= c, 1, 0)) & (cj < c2)
    strict = same_blk & (rl > cl)
    incl = same_blk & (rl >= cl)
    zeros_c = jnp.zeros((c, LANES), F32)
    n_steps = max(1, (c - 1).bit_length())

    pairs = range(hpb)
    sls = [slice(p * LANES, (p + 1) * LANES) for p in pairs]
    each = lambda f, *cols: [f(*xs) for xs in zip(*cols)]
    cat = lambda *xs: jnp.concatenate(xs, axis=0)
    e_sum_b = e_sum.astype(BF16)
    e_mean_b = e_mean.astype(BF16)
    tril_b = tril_incl.astype(BF16)

    r = [r_ref[:, sl] for sl in sls]
    lw = [lw_ref[:, sl] for sl in sls]
    k = [k_ref[:, sl] for sl in sls]
    v = [v_ref[:, sl] for sl in sls]
    kkr = [kk_ref[:, sl] for sl in sls]
    a = [a_ref[:, sl] for sl in sls]
    s = [s_scr[p] for p in pairs]

    ss = each(lambda x: _dot_split(x * x, e_sum_b), kkr)
    kk = each(lambda x, q: x / jnp.maximum(jnp.sqrt(q), 1e-12), kkr, ss)
    cum = each(lambda x: _dot_split_rhs(tril_b, x), lw)
    p_incl = each(jnp.exp, cum)
    p_excl = each(lambda x, y_: jnp.exp(x - y_), cum, lw)
    p_inv = each(lambda x: jnp.exp(-x), cum)
    at = each(lambda x, y_: -x * y_, kk, p_excl)
    rt = each(lambda x, y_: x * y_, r, p_incl)
    bt = each(lambda x, y_, z: x * y_ * z, kk, a, p_inv)
    kt = each(lambda x, y_: x * y_, k, p_inv)

    lhs4 = each(lambda x, y_: cat(jnp.where(m0, x, 0.0), jnp.where(m0, 0.0, x),
                                  jnp.where(m0, y_, 0.0), jnp.where(m0, 0.0, y_)), at, rt)
    m1b = each(lambda x, y_: _dot_nt(x, _pad_rows(cat(y_, y_), LANES)), lhs4, bt)
    m1k = each(lambda x, y_: _dot_nt(x, _pad_rows(cat(y_, y_), LANES)), lhs4, kt)
    m2 = each(lambda x, y_, z: _dot_nt(cat(x, y_), z), at, rt, s)
    v2 = each(lambda x: _pad_rows(cat(x, x), LANES), v)
    rhs = each(lambda x, y_, z: cat(x[:c], x[:c]) + _dot(jnp.where(strict, y_[:c2], 0.0), z), m2, m1k, v2)
    u = each(lambda x: _pad_rows(x, LANES), rhs)
    pw = each(lambda x: _pad_rows(jnp.where(strict, x[:c2], 0.0), LANES), m1b)
    for step in range(n_steps):
        u = each(lambda x, y_: x + _dot(y_, x), u, pw)
        if step + 1 < n_steps:
            pw = each(lambda x: _dot(x, x), pw)
    u_pair = each(lambda x: jnp.where(m0, x[:c], x[c:c2]), u)
    u2 = each(lambda x: _pad_rows(cat(x, x), LANES), u_pair)
    yst = each(lambda xb, xk, uu, vv: _dot(jnp.where(incl, xb[c2:], 0.0), uu) + _dot(jnp.where(incl, xk[c2:], 0.0), vv),
               m1b, m1k, u2, v2)
    y = each(lambda x, y_: x[c:] + jnp.where(m0, y_[:c], y_[c:]), m2, yst)

    p_end = each(lambda x: x[c - 1:c, :], p_incl)
    xt = each(lambda x, y_: _pad_rows(cat(x, y_), LANES), u_pair, v)
    yk = each(lambda x, y_, z: _pad_rows(cat(x * z, y_ * z), LANES), bt, kt, p_end)
    s_new = each(lambda x, z, xx, yy: x * z + _dot(xx.T, yy), s, p_end, xt, yk)
    for p in pairs:
        s_scr[p] = jnp.where(same_head, s_new[p], 0.0)

    mean = each(lambda x: _dot_split(x, e_mean_b), y)
    d = each(lambda x, y_: x - y_, y, mean)
    var = each(lambda x: _dot_split(x * x, e_mean_b), d)
    bonus = each(lambda x, y_, sl: _dot_split(x * y_ * rk_ref[:, sl], e_sum_b), r, k, sls)
    for p in pairs:
        sl = sls[p]
        yn = d[p] * lax.rsqrt(var[p] + GN_EPS) * lnw_ref[:, sl] + lnb_ref[:, sl]
        y_ref[:, sl] = ((yn + bonus[p] * v[p]) * g_ref[:, sl]).astype(y_ref.dtype)

    @pl.when(ci == pl.num_programs(2) - 1)
    def _():
        so_ref[0] = s_scr[...]


def rw_scan(r, lw, k, v, kk, a, g, r_k, ln_w, ln_b, s0, *, nseq, seqlen, chunk, hpb, y_dtype):
    nc = seqlen // chunk
    npair = RW_HEADS // 2
    width = hpb * LANES
    row = lambda b, h, t: (b * nc + t, h)
    par = lambda b, h, t: (0, h)
    st = lambda b, h, t: (b, h, 0, 0)
    kern = functools.partial(_scan_kernel, chunk=chunk, hpb=hpb)
    return pl.pallas_call(
        kern,
        grid=(nseq, npair // hpb, nc),
        in_specs=[pl.BlockSpec((chunk, width), row)] * 7 + [pl.BlockSpec((1, width), par)] * 3
                 + [pl.BlockSpec((1, hpb, LANES, LANES), st)],
        out_specs=[pl.BlockSpec((chunk, width), row), pl.BlockSpec((1, hpb, LANES, LANES), st)],
        out_shape=[jax.ShapeDtypeStruct((nseq * seqlen, RW_WIDTH), y_dtype),
                   jax.ShapeDtypeStruct((nseq, npair, LANES, LANES), F32)],
        scratch_shapes=[pltpu.VMEM((hpb, LANES, LANES), F32)],
        compiler_params=_cparams(("parallel", "parallel", "arbitrary"), 32),
    )(r, lw, k, v, kk, a, g, r_k, ln_w, ln_b, s0)


def _pair_states(s):
    n = s.shape[0]
    s = s.reshape(n, RW_HEADS // 2, 2, RW_HEAD, RW_HEAD)
    z = jnp.zeros_like(s[:, :, 0])
    top = jnp.concatenate([s[:, :, 0], z], axis=-1)
    bot = jnp.concatenate([z, s[:, :, 1]], axis=-1)
    return jnp.concatenate([top, bot], axis=-2)


def _unpair_states(sp):
    n = sp.shape[0]
    s0 = sp[:, :, :RW_HEAD, :RW_HEAD]
    s1 = sp[:, :, RW_HEAD:, RW_HEAD:]
    return jnp.stack([s0, s1], axis=2).reshape(n, RW_HEADS, RW_HEAD, RW_HEAD)


def _kidx_kernel(x_ref, g_ref, kn_ref, kd_ref):
    x = x_ref[...]
    lane = lax.broadcasted_iota(I32, x.shape, 1)
    xk = jnp.where(lane < IDX_DIM, x, 0.0)
    ms = jnp.sum(xk * xk, axis=-1, keepdims=True) * (1.0 / IDX_DIM)
    xd = xk + pltpu.roll(xk, IDX_DIM, 1)
    kd = (xd * lax.rsqrt(ms + RMS_EPS)) * g_ref[...]
    kd_ref[...] = kd
    kn_ref[...] = kd[:, :IDX_DIM]


def kidx_norm(idx_cols, g, *, tm):
    m = idx_cols.shape[0]
    g2 = jnp.concatenate([g, g]).reshape(1, LANES)
    return pl.pallas_call(
        _kidx_kernel,
        grid=(m // tm,),
        in_specs=[pl.BlockSpec((tm, LANES), lambda i: (i, IDX_HEADS * IDX_DIM // LANES)),
                  pl.BlockSpec((1, LANES), lambda i: (0, 0))],
        out_specs=[pl.BlockSpec((tm, IDX_DIM), lambda i: (i, 0)), pl.BlockSpec((tm, LANES), lambda i: (i, 0))],
        out_shape=[jax.ShapeDtypeStruct((m, IDX_DIM), F32), jax.ShapeDtypeStruct((m, LANES), F32)],
        compiler_params=_cparams(("parallel",), 16),
    )(idx_cols, g2)


def _sort_key(score):
    score = jnp.where(score == 0.0, 0.0, score)
    bits = pltpu.bitcast(score, I32)
    return bits ^ (lax.shift_right_arithmetic(bits, 31) & 0x7FFFFFFF)


def _count_lanes(acc):
    return jnp.dot(acc.astype(BF16), jnp.ones((LANES, LANES), BF16), preferred_element_type=F32)


def _topk_select(key_tile, cut_ref, *, rows, n_tiles, topk, col_bits):
    def count(pred):
        acc = jnp.zeros((rows, LANES), F32)
        for t in range(n_tiles):
            acc = acc + jnp.where(pred(key_tile(t), t), 1.0, 0.0)
        return _count_lanes(acc)

    def bit_step(it, thr):
        cand = thr + lax.shift_left(jnp.int32(1), 31 - it)
        cnt = count(lambda kt, t: kt >= cand)
        return jnp.where(cnt >= topk, cand, thr)

    thr = lax.fori_loop(0, 32, bit_step, jnp.full((rows, LANES), INT_MIN, I32))
    n_gt = count(lambda kt, t: kt > thr)
    n_ge = count(lambda kt, t: kt >= thr)
    excess = (n_ge > topk) & (thr > NEG_INF_KEY)
    cut_ref[...] = jnp.full((rows, LANES), BIG_COL, I32)
    lane = lax.broadcasted_iota(I32, (rows, LANES), 1)

    @pl.when(jnp.max(jnp.where(excess, 1.0, 0.0)) > 0.0)
    def _():
        need = topk - n_gt

        def col_step(it, cut):
            cand = cut + lax.shift_left(jnp.int32(1), col_bits - 1 - it)
            cnt = count(lambda kt, t: (kt == thr) & (lane + t * LANES < cand))
            return jnp.where(cnt < need, cand, cut)

        cut = lax.fori_loop(0, col_bits, col_step, jnp.zeros((rows, LANES), I32))
        cut_ref[...] = jnp.where(excess, cut, BIG_COL)

    return thr


def _selected(kt, t, thr, cut):
    lane = lax.broadcasted_iota(I32, kt.shape, 1)
    return ((kt > thr) | ((kt == thr) & (lane + t * LANES <= cut))) & (kt != NEG_INF_KEY)


def _idx_scores(q_ref, wq_ref, kd, *, rows):
    lane = lax.broadcasted_iota(I32, (rows, LANES), 1)
    lo = lane < IDX_DIM
    k_hi, k_lo = _split_bf16(kd)
    k_cat = jnp.concatenate([k_hi, k_lo], axis=1)
    zero = jnp.zeros((rows, LANES), BF16)
    score = jnp.zeros((rows, kd.shape[0]), F32)
    for hp in range(IDX_HEADS // 2):
        qp = q_ref[:, hp * LANES:(hp + 1) * LANES]
        q_hi = qp.astype(BF16)
        q_lo = pltpu.roll(qp - q_hi.astype(F32), IDX_DIM, 1).astype(BF16)
        for half in range(2):
            h = 2 * hp + half
            own = lo if half == 0 else jnp.logical_not(lo)
            lhs = jnp.concatenate([jnp.where(own, q_hi, q_lo), jnp.where(own, q_hi, zero)], axis=1)
            d = lax.dot_general(lhs, k_cat, (((1,), (1,)), ((), ())), preferred_element_type=F32)
            wh = wq_ref[:, IDX_DIM + h:IDX_DIM + h + 1] * IDX_SCALE
            score = score + wh * jnp.maximum(d, 0.0)
    return score


def _dot_nt3(a, b_hi, b_lo):
    a_hi, a_lo = _split_bf16(a)
    nt = lambda x, y: lax.dot_general(x, y, (((1,), (1,)), ((), ())), preferred_element_type=F32)
    return nt(a_hi, b_hi) + (nt(a_hi, b_lo) + nt(a_lo, b_hi))


def _prompt_index_kernel(q_ref, wq_ref, kd_ref, m_ref, key_scr, cut_scr, *, tq, q0, kv, n_tiles, topk):
    cols = n_tiles * LANES
    q_pos = q0 + lax.broadcasted_iota(I32, (tq, cols), 0)
    score = _idx_scores(q_ref, wq_ref, kd_ref[0], rows=tq)
    col = lax.broadcasted_iota(I32, (tq, cols), 1)
    key_scr[...] = _sort_key(jnp.where(col <= q_pos, score, -jnp.inf))
    key_tile = lambda t: key_scr[:, t * LANES:(t + 1) * LANES]
    thr = _topk_select(key_tile, cut_scr, rows=tq, n_tiles=n_tiles, topk=topk, col_bits=12)
    cut = cut_scr[...]
    for t in range(n_tiles):
        width = min(LANES, kv - t * LANES)
        sel = jnp.where(_selected(key_tile(t), t, thr, cut), 1.0, 0.0).astype(m_ref.dtype)
        m_ref[:, t * LANES:t * LANES + width] = sel[:, :width]


def prompt_index_mask(idx_cols, kdup_pad, *, nseq, seqlen, tq, qi, topk):
    nq = seqlen // tq
    kv = (qi + 1) * tq
    n_tiles = -(-kv // LANES)
    cols = n_tiles * LANES
    assert cols <= kdup_pad.shape[1]
    kern = functools.partial(_prompt_index_kernel, tq=tq, q0=qi * tq, kv=kv, n_tiles=n_tiles, topk=topk)
    return pl.pallas_call(
        kern,
        grid=(nseq,),
        in_specs=[pl.BlockSpec((tq, IDX_HEADS * IDX_DIM), lambda b: (b * nq + qi, 0)),
                  pl.BlockSpec((tq, LANES), lambda b: (b * nq + qi, IDX_HEADS * IDX_DIM // LANES)),
                  pl.BlockSpec((1, cols, LANES), lambda b: (b, 0, 0))],
        out_specs=pl.BlockSpec((tq, kv), lambda b: (b, 0)),
        out_shape=jax.ShapeDtypeStruct((nseq * tq, kv), BF16),
        scratch_shapes=[pltpu.VMEM((tq, cols), I32), pltpu.VMEM((tq, LANES), I32)],
        compiler_params=_cparams(("parallel",)),
    )(idx_cols, idx_cols, kdup_pad)


def _prompt_attn_kernel(q_ref, k_ref, v_ref, m_ref, o_ref):
    logits = _dot_nt(q_ref[...], k_ref[0]) * ATT_SCALE
    logits = jnp.where(m_ref[...] > 0, logits, -jnp.inf)
    mx = jnp.max(logits, axis=-1, keepdims=True)
    e = jnp.exp(logits - mx)
    prob = e * (1.0 / jnp.sum(e, axis=-1, keepdims=True))
    o_ref[0] = _dot(prob, v_ref[0]).astype(o_ref.dtype)


def prompt_attention(qkv, mask, *, nseq, seqlen, tq, qi):
    nq = seqlen // tq
    kv = (qi + 1) * tq
    qkv3 = qkv.reshape(nseq, seqlen, 3 * ATT_WIDTH)
    return pl.pallas_call(
        _prompt_attn_kernel,
        grid=(nseq, ATT_HEADS),
        in_specs=[pl.BlockSpec((tq, ATT_HEAD), lambda b, h: (b * nq + qi, h)),
                  pl.BlockSpec((1, kv, ATT_HEAD), lambda b, h: (b, 0, ATT_HEADS + h)),
                  pl.BlockSpec((1, kv, ATT_HEAD), lambda b, h: (b, 0, 2 * ATT_HEADS + h)),
                  pl.BlockSpec((tq, kv), lambda b, h: (b, 0))],
        out_specs=pl.BlockSpec((1, tq, ATT_HEAD), lambda b, h: (b, 0, h)),
        out_shape=jax.ShapeDtypeStruct((nseq, tq, ATT_WIDTH), BF16),
        compiler_params=_cparams(("parallel", "arbitrary")),
    )(qkv, qkv3, qkv3, mask)


PAGES_PER_STEP = 8


def _sample_score_kernel(pt_ref, q_ref, w_ref, *rest, dsq):
    page_refs, o_ref = rest[:-1], rest[-1]
    q = q_ref[0]
    w = w_ref[0]
    for j, page_ref in enumerate(page_refs):
        p_hi, p_lo = _split_bf16(page_ref[0, 0])
        d = jnp.maximum(_dot_nt3(q, p_hi, p_lo), 0.0) * w
        o_ref[0, :, j * PAGE_SIZE:(j + 1) * PAGE_SIZE] = jnp.sum(d.reshape(IDX_HEADS, dsq, PAGE_SIZE), axis=0)


def sample_page_scores(page_table, qh, wh, cache_kidx, layer, *, nb, dsq, n_pages):
    kern = functools.partial(_sample_score_kernel, dsq=dsq)
    rows = IDX_HEADS * dsq
    pps = PAGES_PER_STEP
    assert n_pages % pps == 0

    def page_spec(j):
        return pl.BlockSpec((1, 1, PAGE_SIZE, IDX_DIM),
                            lambda b, p, pt: (layer, pt[b * n_pages + p * pps + j], 0, 0))

    grid_spec = pltpu.PrefetchScalarGridSpec(
        num_scalar_prefetch=1,
        grid=(nb, n_pages // pps),
        in_specs=[pl.BlockSpec((1, rows, IDX_DIM), lambda b, p, pt: (b, 0, 0)),
                  pl.BlockSpec((1, rows, 1), lambda b, p, pt: (b, 0, 0))] + [page_spec(j) for j in range(pps)],
        out_specs=pl.BlockSpec((1, dsq, pps * PAGE_SIZE), lambda b, p, pt: (b, 0, p)),
    )
    return pl.pallas_call(
        kern,
        grid_spec=grid_spec,
        out_shape=jax.ShapeDtypeStruct((nb, dsq, n_pages * PAGE_SIZE), F32),
        compiler_params=_cparams(("parallel", "arbitrary"), 16),
    )(page_table.reshape(-1), qh, wh, *([cache_kidx] * pps))


def _sample_select_kernel(sc_ref, q_ref, w_ref, kn_ref, idx_ref, key_scr, keyq_scr, rank_scr, cut_scr,
                          *, dsq, past, topk, nt_pad):
    n_past_tiles = past // LANES
    n_tiles = n_past_tiles + 1
    keyq_scr[...] = jnp.full(keyq_scr.shape, NEG_INF_KEY, I32)

    def put_tile(t, keys):
        key_scr[t] = keys
        for q in range(dsq):
            keyq_scr[q, t:t + 1, :] = keys[q:q + 1, :]

    for t in range(n_past_tiles):
        put_tile(t, _sort_key(sc_ref[0, :, t * LANES:(t + 1) * LANES]))
    kn_hi, kn_lo = _split_bf16(kn_ref[0])
    d = jnp.maximum(_dot_nt3(q_ref[0], kn_hi, kn_lo), 0.0) * w_ref[0]
    s_new = jnp.sum(d.reshape(IDX_HEADS, dsq, LANES), axis=0)
    qrow = lax.broadcasted_iota(I32, (dsq, LANES), 0)
    jcol = lax.broadcasted_iota(I32, (dsq, LANES), 1)
    put_tile(n_past_tiles, _sort_key(jnp.where(jcol <= qrow, s_new, -jnp.inf)))
    thr = _topk_select(lambda t: key_scr[t], cut_scr, rows=dsq, n_tiles=n_tiles, topk=topk, col_bits=15)
    cut = cut_scr[...]

    li = lax.broadcasted_iota(I32, (LANES, LANES), 0)
    lj = lax.broadcasted_iota(I32, (LANES, LANES), 1)
    upper = jnp.where(li < lj, 1.0, 0.0).astype(BF16)
    ones = jnp.ones((LANES, LANES), BF16)
    ti = lax.broadcasted_iota(I32, (nt_pad, nt_pad), 0)
    tj = lax.broadcasted_iota(I32, (nt_pad, nt_pad), 1)
    earlier = jnp.where(tj < ti, 1.0, 0.0).astype(BF16)
    col = (lax.broadcasted_iota(I32, (nt_pad, LANES), 0) * LANES + lax.broadcasted_iota(I32, (nt_pad, LANES), 1))
    slot_f = lax.broadcasted_iota(I32, (topk, LANES), 0).astype(F32)
    lane_f = lax.broadcasted_iota(I32, (topk, LANES), 1).astype(F32)
    for q in range(dsq):
        kq = keyq_scr[q]
        thr_q = thr[q:q + 1]
        cut_q = cut[q:q + 1]
        sel = ((kq > thr_q) | ((kq == thr_q) & (col <= cut_q))) & (kq != NEG_INF_KEY)
        selb = jnp.where(sel, 1.0, 0.0).astype(BF16)
        within = jnp.dot(selb, upper, preferred_element_type=F32)
        totals = jnp.dot(selb, ones, preferred_element_type=F32)
        before = jnp.dot(earlier, totals.astype(BF16), preferred_element_type=F32)
        rank_scr[...] = jnp.where(sel, before + within, -1.0)

        def tile_step(t, acc):
            rk = jnp.broadcast_to(rank_scr[pl.ds(t, 1), :], (topk, LANES))
            return acc + jnp.where(rk == slot_f, lane_f + jnp.asarray(t * LANES).astype(F32), 0.0)

        acc = lax.fori_loop(0, n_tiles, tile_step, jnp.zeros((topk, LANES), F32))
        idx_ref[0, q] = jnp.sum(acc, axis=1, keepdims=True).astype(I32)


def sample_select(scores, qh, wh, kn_pad, *, nb, dsq, past, topk):
    rows = IDX_HEADS * dsq
    n_tiles = past // LANES + 1
    nt_pad = -(-n_tiles // LANES) * LANES
    kern = functools.partial(_sample_select_kernel, dsq=dsq, past=past, topk=topk, nt_pad=nt_pad)
    return pl.pallas_call(
        kern,
        grid=(nb,),
        in_specs=[pl.BlockSpec((1, dsq, past), lambda b: (b, 0, 0)),
                  pl.BlockSpec((1, rows, IDX_DIM), lambda b: (b, 0, 0)),
                  pl.BlockSpec((1, rows, 1), lambda b: (b, 0, 0)),
                  pl.BlockSpec((1, LANES, IDX_DIM), lambda b: (b, 0, 0))],
        out_specs=pl.BlockSpec((1, dsq, topk, 1), lambda b: (b, 0, 0, 0)),
        out_shape=jax.ShapeDtypeStruct((nb, dsq, topk, 1), I32),
        scratch_shapes=[pltpu.VMEM((n_tiles, dsq, LANES), I32), pltpu.VMEM((dsq, nt_pad, LANES), I32),
                        pltpu.VMEM((nt_pad, LANES), F32), pltpu.VMEM((dsq, LANES), I32)],
        compiler_params=_cparams(("parallel",), 32),
    )(scores, qh, wh, kn_pad)


def _sample_attn_kernel(idx_ref, pt_ref, q_ref, kn_ref, vn_ref, ck_ref, cv_ref, o_ref,
                        kbuf, vbuf, sem, *, layer, dsq, past, n_pages, topk):
    b = pl.program_id(0)

    def row_copies(q, j, slot):
        pos = idx_ref[(b * dsq + q) * topk + j]
        in_past = pos < past
        ppos = jnp.minimum(pos, past - 1)
        page = pt_ref[b * n_pages + ppos // PAGE_SIZE]
        off = ppos % PAGE_SIZE
        npos = jnp.clip(pos - past, 0, dsq - 1)
        return in_past, (
            pltpu.make_async_copy(ck_ref.at[layer, page, off], kbuf.at[slot, :, j], sem.at[0, slot]),
            pltpu.make_async_copy(cv_ref.at[layer, page, off], vbuf.at[slot, :, j], sem.at[1, slot]),
            pltpu.make_async_copy(kn_ref.at[b, npos], kbuf.at[slot, :, j], sem.at[0, slot]),
            pltpu.make_async_copy(vn_ref.at[b, npos], vbuf.at[slot, :, j], sem.at[1, slot]))

    def start_query(q, slot):
        def body(j, carry):
            in_past, (ck, cv, nk, nv) = row_copies(q, j, slot)

            @pl.when(in_past)
            def _():
                ck.start()
                cv.start()

            @pl.when(jnp.logical_not(in_past))
            def _():
                nk.start()
                nv.start()

            return carry

        lax.fori_loop(0, topk, body, 0)

    def wait_query(slot):
        def body(j, carry):
            pltpu.make_async_copy(kn_ref.at[0, 0], kbuf.at[slot, :, j], sem.at[0, slot]).wait()
            pltpu.make_async_copy(vn_ref.at[0, 0], vbuf.at[slot, :, j], sem.at[1, slot]).wait()
            return carry

        lax.fori_loop(0, topk, body, 0)

    start_query(0, 0)
    for q in range(dsq):
        slot = q % 2
        if q + 1 < dsq:
            start_query(q + 1, 1 - slot)
        wait_query(slot)
        outs = []
        for h in range(ATT_HEADS):
            qh = jnp.broadcast_to(q_ref[0, q, h:h + 1, :], (8, ATT_HEAD))
            logits = _dot_nt(qh, kbuf[slot, h])[0:1] * ATT_SCALE
            mx = jnp.max(logits, axis=-1, keepdims=True)
            e = jnp.exp(logits - mx)
            prob = e / jnp.sum(e, axis=-1, keepdims=True)
            outs.append(_dot(jnp.broadcast_to(prob, (8, topk)), vbuf[slot, h])[0:1])
        o_ref[0, q:q + 1, :] = jnp.concatenate(outs, axis=1)


def sample_attention(sel_idx, page_table, q, k_new, v_new, cache_k, cache_v, layer, *, nb, dsq, past, topk):
    n_pages = past // PAGE_SIZE
    kern = functools.partial(_sample_attn_kernel, layer=layer, dsq=dsq, past=past, n_pages=n_pages, topk=topk)
    grid_spec = pltpu.PrefetchScalarGridSpec(
        num_scalar_prefetch=2,
        grid=(nb,),
        in_specs=[pl.BlockSpec((1, dsq, ATT_HEADS, ATT_HEAD), lambda b, idx, pt: (b, 0, 0, 0)),
                  pl.BlockSpec(memory_space=pl.ANY), pl.BlockSpec(memory_space=pl.ANY),
                  pl.BlockSpec(memory_space=pl.ANY), pl.BlockSpec(memory_space=pl.ANY)],
        out_specs=pl.BlockSpec((1, dsq, ATT_WIDTH), lambda b, idx, pt: (b, 0, 0)),
        scratch_shapes=[pltpu.VMEM((2, ATT_HEADS, topk, ATT_HEAD), F32),
                        pltpu.VMEM((2, ATT_HEADS, topk, ATT_HEAD), F32),
                        pltpu.SemaphoreType.DMA((2, 2))],
    )
    return pl.pallas_call(
        kern,
        grid_spec=grid_spec,
        out_shape=jax.ShapeDtypeStruct((nb, dsq, ATT_WIDTH), F32),
        compiler_params=_cparams(("arbitrary",), 32),
    )(sel_idx.reshape(-1), page_table.reshape(-1), q, k_new, v_new, cache_k, cache_v)


def _conv_gate_kernel(ug_ref, uv_ref, pg_ref, pv_ref, wg_ref, wv_ref, bg_ref, bv_ref, o_ref, *, nb, seqlen):
    def conv(u, prev, w, bias):
        return w[0:1, :] * _shift_rows(u, prev, 2) + w[1:2, :] * _shift_rows(u, prev[1:2, :], 1) + w[2:3, :] * u + bias

    outs = []
    for s in range(nb):
        rows = slice(s * seqlen, (s + 1) * seqlen)
        gate = conv(ug_ref[rows, :], pg_ref[s], wg_ref[...], bg_ref[...])
        val = conv(uv_ref[rows, :], pv_ref[s], wv_ref[...], bv_ref[...])
        outs.append(gate * jax.nn.sigmoid(gate) * val)
    o_ref[...] = (outs[0] if nb == 1 else jnp.concatenate(outs, axis=0)).astype(o_ref.dtype)


def conv_gate(u, conv_prev, conv_w, conv_b, *, nseq, seqlen, nb, tc):
    m = nseq * seqlen
    half = D_FF // tc
    kern = functools.partial(_conv_gate_kernel, nb=nb, seqlen=seqlen)
    rows = nb * seqlen
    return pl.pallas_call(
        kern,
        grid=(nseq // nb, half),
        in_specs=[pl.BlockSpec((rows, tc), lambda b, j: (b, j)),
                  pl.BlockSpec((rows, tc), lambda b, j: (b, j + half)),
                  pl.BlockSpec((nb, CONV_W - 1, tc), lambda b, j: (b, 0, j)),
                  pl.BlockSpec((nb, CONV_W - 1, tc), lambda b, j: (b, 0, j + half)),
                  pl.BlockSpec((CONV_W, tc), lambda b, j: (0, j)),
                  pl.BlockSpec((CONV_W, tc), lambda b, j: (0, j + half)),
                  pl.BlockSpec((1, tc), lambda b, j: (0, j)),
                  pl.BlockSpec((1, tc), lambda b, j: (0, j + half))],
        out_specs=pl.BlockSpec((rows, tc), lambda b, j: (b, j)),
        out_shape=jax.ShapeDtypeStruct((m, D_FF), BF16),
        compiler_params=_cparams(("parallel", "arbitrary"), 48),
    )(u, u, conv_prev, conv_prev, conv_w, conv_w, conv_b, conv_b)


def _layer_weights(l, w_in, rw_mu, rw_g2):
    wl = w_in[l]
    a0 = RW_COLS
    w_rw = jnp.pad(wl[:, :a0], ((0, 0), (0, RW_PAD - RW_COLS))).astype(BF16)
    w_qkv = wl[:, a0:a0 + 3 * ATT_WIDTH].astype(BF16)
    i0 = a0 + 3 * ATT_WIDTH
    w_idx = jnp.pad(wl[:, i0:a0 + ATT_COLS], ((0, 0), (0, IDX_PAD - (ATT_COLS - 3 * ATT_WIDTH)))).astype(BF16)
    w_gate = wl[:, a0 + ATT_COLS:].astype(BF16)
    mu = jnp.pad(rw_mu[l], (0, RW_PAD - RW_COLS)).reshape(1, RW_PAD)
    g2 = jnp.pad(rw_g2[l], ((0, G_PAD - G_LORA), (0, 0)))
    return w_rw, w_qkv, w_idx, w_gate, mu, g2


def _trunk_layer(x, l, p, lw, shift_prev, wkv_prev, conv_prev, attend, *, nseq, seqlen, tiles):
    m = nseq * seqlen
    w_rw, w_qkv, w_idx, w_gate, mu, g2 = lw
    row = lambda a: a.reshape(1, -1)
    h = rmsnorm_rows(x, p['norm_mix'][l], tm=tiles['rms'], out_dtype=BF16)
    rw = matmul(h, w_rw, tm=tiles['tm'], tn=tiles['tn_rw'])
    qkv = matmul(h, w_qkv, tm=tiles['tm'], tn=tiles['tn_qkv'])
    idx_cols = matmul(h, w_idx, tm=tiles['tm'], tn=tiles['tn_idx'])
    gates = matmul(h, w_gate, tm=tiles['tm'], tn=tiles['tn_gate'])

    sp = jnp.pad(shift_prev, ((0, 0), (0, 0), (0, RW_PAD - RW_COLS)))
    r, lwd, k2, v, kk, a, g = rw_prep(rw, sp, mu, row(p['rw_w0'][l]), p['rw_w2'][l], row(p['rw_a0'][l]),
                                       p['rw_a2'][l], g2, row(p['rw_k_k'][l]), row(p['rw_k_a'][l]),
                                       nseq=nseq, seqlen=seqlen, tt=tiles['chunk'])
    y_a, s_pair = rw_scan(r, lwd, k2, v, kk, a, g, row(p['rw_r_k'][l]), row(p['rw_ln_w'][l]), row(p['rw_ln_b'][l]),
                          _pair_states(wkv_prev), nseq=nseq, seqlen=seqlen, chunk=tiles['chunk'], hpb=tiles['hpb'],
                          y_dtype=tiles['y_dtype'])
    y_a = y_a.astype(BF16)
    wkv_new = _unpair_states(s_pair)
    shift_new = rw.reshape(nseq, seqlen, RW_PAD)[:, seqlen - 1:, :RW_COLS]

    ki_n, ki_dup = kidx_norm(idx_cols, p['idx_k_norm'][l], tm=tiles['kidx'])
    y_b = attend(qkv, idx_cols, ki_n, ki_dup)

    mix = branch_mix(y_a, y_b, p['w_branch_a'][l], p['w_branch_b'][l], gates, tm=tiles['tm'], tn=tiles['tn_mix'])
    x = matmul(mix, p['w_out'][l], tm=tiles['tm'], tn=tiles['tn_out'], residual=x)

    h2 = rmsnorm_rows(x, p['norm_ffn'][l], tm=tiles['rms'], out_dtype=BF16)
    u = matmul(h2, p['w_up'][l], tm=tiles['tm'], tn=tiles['tn_up'])
    act = conv_gate(u, conv_prev, p['conv_w'][l], row(p['conv_b'][l]), nseq=nseq, seqlen=seqlen,
                    nb=tiles['conv_nb'], tc=tiles['conv_tc'])
    u3 = u.reshape(nseq, seqlen, 2 * D_FF)
    conv_new = jnp.concatenate([conv_prev, u3], axis=1)[:, seqlen:] if seqlen < CONV_W - 1 else u3[:, seqlen - (CONV_W - 1):]
    x = matmul(act, p['w_down_bf16'][l], tm=tiles['tm_down'], tn=tiles['tn_down'], residual=x)

    k_new = qkv[:, ATT_WIDTH:2 * ATT_WIDTH].reshape(nseq, seqlen, ATT_HEADS, ATT_HEAD)
    v_new = qkv[:, 2 * ATT_WIDTH:].reshape(nseq, seqlen, ATT_HEADS, ATT_HEAD)
    return x, (k_new, v_new, ki_n.reshape(nseq, seqlen, IDX_DIM), wkv_new, shift_new, conv_new)


PROMPT_TILES = dict(rms=688, tm=1376, tn_rw=768, tn_qkv=768, tn_idx=384, tn_gate=512, tn_mix=256, tn_out=256, tn_up=512,
                    tm_down=688, tn_down=256, chunk=48, hpb=8, kidx=688, conv_nb=1, conv_tc=256, tq=688, y_dtype=BF16)
SAMPLE_TILES = dict(rms=64, tm=64, tn_rw=1152, tn_qkv=1024, tn_idx=1152, tn_gate=1024, tn_mix=1024, tn_out=1024, tn_up=512,
                    tm_down=64, tn_down=512, chunk=8, hpb=4, kidx=64, conv_nb=8, conv_tc=256, y_dtype=F32)


def kernel(x_prompt, x_sample, cache_k, cache_v, cache_kidx, state_wkv, state_shift, state_conv, page_table,
           meta_tokens, norm_mix, w_in, rw_mu, rw_w0, rw_w2, rw_a0, rw_a2, rw_g2, rw_k_k, rw_k_a, rw_r_k,
           rw_ln_w, rw_ln_b, idx_k_norm, w_branch_a, w_branch_b, w_out, norm_ffn, w_up, conv_w, conv_b,
           w_down, norm_final):
    p = {'norm_mix': norm_mix, 'rw_w0': rw_w0, 'rw_w2': rw_w2, 'rw_a0': rw_a0, 'rw_a2': rw_a2,
         'rw_k_k': rw_k_k, 'rw_k_a': rw_k_a, 'rw_r_k': rw_r_k.reshape(rw_r_k.shape[0], RW_WIDTH),
         'rw_ln_w': rw_ln_w, 'rw_ln_b': rw_ln_b, 'idx_k_norm': idx_k_norm, 'w_branch_a': w_branch_a,
         'w_branch_b': w_branch_b, 'w_out': w_out, 'norm_ffn': norm_ffn, 'w_up': w_up, 'conv_w': conv_w,
         'conv_b': conv_b, 'w_down_bf16': w_down.astype(BF16)}
    depth = w_in.shape[0]
    nb_p, seq = x_prompt.shape[:2]
    tp = seq + N_META
    nb_s, ds = x_sample.shape[:2]
    n_pages = page_table.shape[1]
    past = n_pages * PAGE_SIZE

    meta = jnp.broadcast_to(meta_tokens[None], (nb_p, N_META, D_MODEL))
    xp = jnp.concatenate([meta, x_prompt], axis=1).reshape(nb_p * tp, D_MODEL)
    xs = x_sample.reshape(nb_s * ds, D_MODEL)

    topk_p = min(TOPK_MAX, tp // 4)
    topk_s = min(TOPK_MAX, (past + ds) // 4)
    lp = -(-tp // LANES) * LANES
    tq = PROMPT_TILES['tq']

    def prompt_attend(qkv, idx_cols, ki_n, ki_dup):
        kd = jnp.pad(ki_dup.reshape(nb_p, tp, LANES), ((0, 0), (0, lp - tp), (0, 0)))
        outs = []
        for qi in range(tp // tq):
            mask = prompt_index_mask(idx_cols, kd, nseq=nb_p, seqlen=tp, tq=tq, qi=qi, topk=topk_p)
            outs.append(prompt_attention(qkv, mask, nseq=nb_p, seqlen=tp, tq=tq, qi=qi))
        return jnp.concatenate(outs, axis=1).reshape(nb_p * tp, ATT_WIDTH)

    def make_sample_attend(l):
        def attend(qkv, idx_cols, ki_n, ki_dup):
            qi = idx_cols[:, :IDX_HEADS * IDX_DIM].reshape(nb_s, ds, IDX_HEADS, IDX_DIM)
            qh = jnp.swapaxes(qi, 1, 2).reshape(nb_s, IDX_HEADS * ds, IDX_DIM)
            wi = idx_cols[:, IDX_HEADS * IDX_DIM + IDX_DIM:IDX_HEADS * IDX_DIM + IDX_DIM + IDX_HEADS]
            wh = jnp.swapaxes(wi.reshape(nb_s, ds, IDX_HEADS), 1, 2).reshape(nb_s, IDX_HEADS * ds, 1) * IDX_SCALE
            scores = sample_page_scores(page_table, qh, wh, cache_kidx, l, nb=nb_s, dsq=ds, n_pages=n_pages)
            kn_pad = jnp.pad(ki_n.reshape(nb_s, ds, IDX_DIM), ((0, 0), (0, LANES - ds), (0, 0)))
            sel = sample_select(scores, qh, wh, kn_pad, nb=nb_s, dsq=ds, past=past, topk=topk_s)
            q = qkv[:, :ATT_WIDTH].reshape(nb_s, ds, ATT_HEADS, ATT_HEAD)
            k_new = qkv[:, ATT_WIDTH:2 * ATT_WIDTH].reshape(nb_s, ds, ATT_HEADS, ATT_HEAD)
            v_new = qkv[:, 2 * ATT_WIDTH:].reshape(nb_s, ds, ATT_HEADS, ATT_HEAD)
            y = sample_attention(sel, page_table, q, k_new, v_new, cache_k, cache_v, l,
                                 nb=nb_s, dsq=ds, past=past, topk=topk_s)
            return y.reshape(nb_s * ds, ATT_WIDTH).astype(BF16)
        return attend

    zero_shift = jnp.zeros((nb_p, 1, RW_COLS), F32)
    zero_wkv = jnp.zeros((nb_p, RW_HEADS, RW_HEAD, RW_HEAD), F32)
    zero_conv = jnp.zeros((nb_p, CONV_W - 1, 2 * D_FF), F32)
    sts_p, sts_s = [], []
    for l in range(depth):
        lw = _layer_weights(l, w_in, rw_mu, rw_g2)
        xp, st = _trunk_layer(xp, l, p, lw, zero_shift, zero_wkv, zero_conv, prompt_attend,
                              nseq=nb_p, seqlen=tp, tiles=PROMPT_TILES)
        sts_p.append(st)
        xs, st = _trunk_layer(xs, l, p, lw, state_shift[l], state_wkv[l], state_conv[l], make_sample_attend(l),
                              nseq=nb_s, seqlen=ds, tiles=SAMPLE_TILES)
        sts_s.append(st)

    y_prompt = rmsnorm_rows(xp, norm_final, tm=PROMPT_TILES['rms'], out_dtype=F32)
    y_prompt = y_prompt.reshape(nb_p, tp, D_MODEL)[:, N_META:]
    y_sample = rmsnorm_rows(xs, norm_final, tm=SAMPLE_TILES['rms'], out_dtype=F32).reshape(nb_s, ds, D_MODEL)
    stk = lambda sts, i: jnp.stack([s[i] for s in sts])
    return (y_prompt, y_sample,
            stk(sts_p, 0), stk(sts_p, 1), stk(sts_p, 2), stk(sts_p, 3), stk(sts_p, 4), stk(sts_p, 5),
            stk(sts_s, 0), stk(sts_s, 1), stk(sts_s, 2), stk(sts_s, 3), stk(sts_s, 4), stk(sts_s, 5))
```

```python
import functools

import jax
import jax.numpy as jnp
from jax import lax
from jax.experimental import pallas as pl
from jax.experimental.pallas import tpu as pltpu

F32 = jnp.float32
BF16 = jnp.bfloat16
I32 = jnp.int32
HIGHEST = lax.Precision.HIGHEST

D_MODEL = 4096
N_META = 16
PAGE_SIZE = 128
RW_HEAD = 64
RW_HEADS = 32
RW_WIDTH = RW_HEADS * RW_HEAD
W_LORA = 128
A_LORA = 128
G_LORA = 480
RW_COLS = 3 * RW_WIDTH + W_LORA + A_LORA + G_LORA
ATT_HEADS = 16
ATT_HEAD = 128
ATT_WIDTH = ATT_HEADS * ATT_HEAD
IDX_HEADS = 16
IDX_DIM = 64
TOPK_MAX = 256
ATT_COLS = 3 * ATT_WIDTH + IDX_HEADS * IDX_DIM + IDX_DIM + IDX_HEADS
D_FF = 11008
CONV_W = 3
RMS_EPS = 1e-6
GN_EPS = 64e-5
ATT_SCALE = ATT_HEAD ** -0.5
IDX_SCALE = (IDX_HEADS ** -0.5) * (IDX_DIM ** -0.5)

LANES = 128
RW_PAD = 6912
IDX_PAD = 1280
G_PAD = 512
INT_MIN = -2 ** 31
NEG_INF_KEY = INT_MIN + 0x7FFFFF
BIG_COL = 2 ** 30
VMEM_MB = 60


def _cparams(sem, mb=VMEM_MB):
    return pltpu.CompilerParams(dimension_semantics=sem, vmem_limit_bytes=mb * 1024 * 1024)


def _dot(a, b):
    return jnp.dot(a.astype(BF16), b.astype(BF16), preferred_element_type=F32)


def _dot_nt(a, b, precision=None):
    if precision is None:
        a, b = a.astype(BF16), b.astype(BF16)
    return lax.dot_general(a, b, (((1,), (1,)), ((), ())), precision=precision,
                           preferred_element_type=F32)


def _dot_hi(a, b):
    return jnp.dot(a, b, precision=HIGHEST, preferred_element_type=F32)


def _split_bf16(x):
    hi = x.astype(BF16)
    return hi, (x - hi.astype(F32)).astype(BF16)


def _dot_split(x, m2_bf16):
    hi, lo = _split_bf16(x)
    return jnp.dot(jnp.concatenate([hi, lo], axis=1), m2_bf16, preferred_element_type=F32)


def _dot_split_rhs(m_bf16, x):
    hi, lo = _split_bf16(x)
    out = jnp.dot(m_bf16, jnp.concatenate([hi, lo], axis=1), preferred_element_type=F32)
    return out[:, :LANES] + out[:, LANES:]


def _rms_kernel(x_ref, g_ref, o_ref):
    x = x_ref[...]
    ms = jnp.sum(x * x, axis=-1, keepdims=True) * (1.0 / x.shape[-1])
    o_ref[...] = ((x * lax.rsqrt(ms + RMS_EPS)) * g_ref[...]).astype(o_ref.dtype)


def rmsnorm_rows(x, g, *, tm, out_dtype):
    m, d = x.shape
    return pl.pallas_call(
        _rms_kernel,
        grid=(m // tm,),
        in_specs=[pl.BlockSpec((tm, d), lambda i: (i, 0)), pl.BlockSpec((1, d), lambda i: (0, 0))],
        out_specs=pl.BlockSpec((tm, d), lambda i: (i, 0)),
        out_shape=jax.ShapeDtypeStruct((m, d), out_dtype),
        compiler_params=_cparams(("parallel",)),
    )(x, g.reshape(1, d))


def _mm_kernel(x_ref, w_ref, o_ref):
    o_ref[...] = _dot(x_ref[...], w_ref[0]).astype(o_ref.dtype)


def _mm_res_kernel(x_ref, w_ref, r_ref, o_ref):
    o_ref[...] = (r_ref[...] + _dot(x_ref[...], w_ref[0])).astype(o_ref.dtype)


def _layer_spec(k, tn, layer, col_blk0=0):
    return pl.BlockSpec((1, k, tn), lambda i, j: (layer, 0, col_blk0 + j))


def matmul(x, w, *, tm, tn, layer=0, col0=0, n_cols=None, out_dtype=F32, residual=None):
    m, k = x.shape
    n = w.shape[2] if n_cols is None else n_cols
    assert m % tm == 0 and n % tn == 0 and col0 % tn == 0, (m, tm, n, tn, col0)
    in_specs = [pl.BlockSpec((tm, k), lambda i, j: (i, 0)), _layer_spec(k, tn, layer, col0 // tn)]
    args = [x, w]
    kern = _mm_kernel
    if residual is not None:
        in_specs.append(pl.BlockSpec((tm, tn), lambda i, j: (i, j)))
        args.append(residual)
        kern = _mm_res_kernel
    return pl.pallas_call(
        kern,
        grid=(m // tm, n // tn),
        in_specs=in_specs,
        out_specs=pl.BlockSpec((tm, tn), lambda i, j: (i, j)),
        out_shape=jax.ShapeDtypeStruct((m, n), out_dtype),
        compiler_params=_cparams(("parallel", "arbitrary")),
    )(*args)


def _mix_kernel(ya_ref, yb_ref, wa_ref, wb_ref, ga_ref, gb_ref, o_ref):
    pa = _dot(ya_ref[...], wa_ref[0])
    pb = _dot(yb_ref[...], wb_ref[0])
    o_ref[...] = (jax.nn.sigmoid(ga_ref[...]) * pa + jax.nn.sigmoid(gb_ref[...]) * pb).astype(o_ref.dtype)


def branch_mix(ya, yb, wa, wb, gates, *, tm, tn, layer):
    m, k = ya.shape
    n = wa.shape[2]
    assert m % tm == 0 and n % tn == 0, (m, tm, n, tn)
    nb = n // tn
    return pl.pallas_call(
        _mix_kernel,
        grid=(m // tm, nb),
        in_specs=[pl.BlockSpec((tm, k), lambda i, j: (i, 0)),
                  pl.BlockSpec((tm, k), lambda i, j: (i, 0)),
                  _layer_spec(k, tn, layer),
                  _layer_spec(k, tn, layer),
                  pl.BlockSpec((tm, tn), lambda i, j: (i, j)),
                  pl.BlockSpec((tm, tn), lambda i, j: (i, j + nb))],
        out_specs=pl.BlockSpec((tm, tn), lambda i, j: (i, j)),
        out_shape=jax.ShapeDtypeStruct((m, n), BF16),
        compiler_params=_cparams(("parallel", "arbitrary")),
    )(ya, yb, wa, wb, gates, gates)


def _shift_rows(x, first_rows, n):
    rolled = pltpu.roll(x, n, 0)
    row = lax.broadcasted_iota(I32, x.shape, 0)
    out = rolled
    for j in range(n):
        out = jnp.where(row == j, first_rows[j:j + 1, :], out)
    return out


def _rw_prep_kernel(x_ref, sp_ref, mu_ref, w0_ref, w2_ref, a0_ref, a2_ref, g2_ref, kk_ref, ka_ref,
                    r_o, lw_o, k_o, v_o, kk_o, a_o, g_o, carry):
    t = pl.program_id(1)
    tt = x_ref.shape[0]

    @pl.when(t == 0)
    def _():
        carry[...] = sp_ref[0]

    x = x_ref[...]
    xprev = _shift_rows(x, carry[...], 1)
    carry[...] = x[tt - 1:tt, :]
    xs = x + (xprev - x) * mu_ref[...]
    w = RW_WIDTH
    r = xs[:, 0:w]
    k = xs[:, w:2 * w]
    v = xs[:, 2 * w:3 * w]
    wd = xs[:, 3 * w:3 * w + W_LORA]
    ad = xs[:, 3 * w + W_LORA:3 * w + W_LORA + A_LORA]
    gd = xs[:, 3 * w + W_LORA + A_LORA:RW_PAD]
    z = -(w0_ref[...] + _dot(jnp.tanh(wd), w2_ref[...]))
    softplus = jnp.maximum(z, 0.0) + jnp.log(1.0 + jnp.exp(-jnp.abs(z)))
    w_log = -softplus - 0.5
    a = jax.nn.sigmoid(a0_ref[...] + _dot(ad, a2_ref[...]))
    r_o[...] = r
    lw_o[...] = -jnp.exp(w_log)
    k_o[...] = k * (1.0 + (a - 1.0) * ka_ref[...])
    v_o[...] = v
    kk_o[...] = k * kk_ref[...]
    a_o[...] = a
    g_o[...] = _dot(jax.nn.sigmoid(gd), g2_ref[...])


def rw_prep(rw, shift_prev, mu, w0, w2, a0, a2, g2, k_k, k_a, *, nseq, seqlen, tt):
    nt = seqlen // tt
    row = lambda b, t: (b * nt + t, 0)
    const = lambda b, t: (0, 0)
    vec = lambda n: pl.BlockSpec((1, n), const)
    out_sds = jax.ShapeDtypeStruct((nseq * seqlen, RW_WIDTH), F32)
    return pl.pallas_call(
        _rw_prep_kernel,
        grid=(nseq, nt),
        in_specs=[pl.BlockSpec((tt, RW_PAD), row),
                  pl.BlockSpec((1, 1, RW_PAD), lambda b, t: (b, 0, 0)),
                  vec(RW_PAD), vec(RW_WIDTH),
                  pl.BlockSpec((W_LORA, RW_WIDTH), const), vec(RW_WIDTH),
                  pl.BlockSpec((A_LORA, RW_WIDTH), const),
                  pl.BlockSpec((G_PAD, RW_WIDTH), const), vec(RW_WIDTH), vec(RW_WIDTH)],
        out_specs=[pl.BlockSpec((tt, RW_WIDTH), row)] * 7,
        out_shape=[out_sds] * 7,
        scratch_shapes=[pltpu.VMEM((1, RW_PAD), F32)],
        compiler_params=_cparams(("arbitrary", "arbitrary"), 48),
    )(rw, shift_prev, mu, w0, w2, a0, a2, g2, k_k, k_a)


def _pad_rows(x, rows):
    if x.shape[0] == rows:
        return x
    return jnp.concatenate([x, jnp.zeros((rows - x.shape[0], x.shape[1]), x.dtype)], axis=0)


def _scan_kernel(r_ref, lw_ref, k_ref, v_ref, kk_ref, a_ref, g_ref, rk_ref, lnw_ref, lnb_ref, s0_ref,
                 y_ref, so_ref, s_scr, *, chunk, hpb):
    c = chunk
    c2 = 2 * c
    ci = pl.program_id(2)

    @pl.when(ci == 0)
    def _():
        s_scr[...] = s0_ref[0]

    lane = lax.broadcasted_iota(I32, (c, LANES), 1)
    m0 = lane < RW_HEAD
    ei = lax.broadcasted_iota(I32, (LANES, LANES), 0)
    ej = lax.broadcasted_iota(I32, (LANES, LANES), 1)
    same_head = lax.shift_right_logical(ei, 6) == lax.shift_right_logical(ej, 6)
    e_sum = jnp.where(same_head, 1.0, 0.0).astype(F32)
    e_mean = e_sum * (1.0 / RW_HEAD)
    ti = lax.broadcasted_iota(I32, (c, c), 0)
    tj = lax.broadcasted_iota(I32, (c, c), 1)
    tril_incl = jnp.where(ti >= tj, 1.0, 0.0).astype(F32)
    ri = lax.broadcasted_iota(I32, (c2, LANES), 0)
    cj = lax.broadcasted_iota(I32, (c2, LANES), 1)
    rl = jnp.where(ri >= c, ri - c, ri)
    cl = jnp.where(cj >= c, cj - c, cj)
    same_blk = (jnp.where(ri >= c, 1, 0) == jnp.where(cj >= c, 1, 0)) & (cj < c2)
    strict = same_blk & (rl > cl)
    incl = same_blk & (rl >= cl)
    zeros_c = jnp.zeros((c, LANES), F32)
    n_steps = max(1, (c - 1).bit_length())

    pairs = range(hpb)
    sls = [slice(p * LANES, (p + 1) * LANES) for p in pairs]
    each = lambda f, *cols: [f(*xs) for xs in zip(*cols)]
    cat = lambda *xs: jnp.concatenate(xs, axis=0)
    e_sum_b = jnp.concatenate([e_sum, e_sum], axis=0).astype(BF16)
    e_mean_b = jnp.concatenate([e_mean, e_mean], axis=0).astype(BF16)
    tril_b = tril_incl.astype(BF16)
    strict2 = jnp.concatenate([strict, strict], axis=1)
    incl2 = jnp.concatenate([incl, incl], axis=1)

    r = [r_ref[:, sl] for sl in sls]
    lw = [lw_ref[:, sl] for sl in sls]
    k = [k_ref[:, sl] for sl in sls]
    v = [v_ref[:, sl] for sl in sls]
    kkr = [kk_ref[:, sl] for sl in sls]
    a = [a_ref[:, sl] for sl in sls]
    s = [s_scr[p] for p in pairs]

    ss = each(lambda x: _dot_split(x * x, e_sum_b), kkr)
    kk = each(lambda x, q: x / jnp.maximum(jnp.sqrt(q), 1e-12), kkr, ss)
    cum = each(lambda x: _dot_split_rhs(tril_b, x), lw)
    p_incl = each(jnp.exp, cum)
    p_excl = each(lambda x, y_: jnp.exp(x - y_), cum, lw)
    p_inv = each(lambda x: jnp.exp(-x), cum)
    at = each(lambda x, y_: -x * y_, kk, p_excl)
    rt = each(lambda x, y_: x * y_, r, p_incl)
    bt = each(lambda x, y_, z: x * y_ * z, kk, a, p_inv)
    kt = each(lambda x, y_: x * y_, k, p_inv)

    lhs4 = each(lambda x, y_: cat(jnp.where(m0, x, 0.0), jnp.where(m0, 0.0, x),
                                  jnp.where(m0, y_, 0.0), jnp.where(m0, 0.0, y_)), at, rt)
    m1 = each(lambda x, yb, yk: _dot_nt(x, cat(_pad_rows(cat(yb, yb), LANES), _pad_rows(cat(yk, yk), LANES))),
              lhs4, bt, kt)
    m1s = each(lambda x: jnp.where(strict2, x[:c2], 0.0), m1)
    m1i = each(lambda x: jnp.where(incl2, x[c2:], 0.0), m1)
    m2 = each(lambda x, y_, z: _dot_nt(cat(x, y_), z), at, rt, s)
    v2 = each(lambda x: _pad_rows(cat(x, x), LANES), v)
    rhs = each(lambda x, y_, z: cat(x[:c], x[:c]) + _dot(y_[:, LANES:], z), m2, m1s, v2)
    u = each(lambda x: _pad_rows(x, LANES), rhs)
    pw = each(lambda x: _pad_rows(x[:, :LANES], LANES), m1s)
    for step in range(n_steps):
        u = each(lambda x, y_: x + _dot(y_, x), u, pw)
        if step + 1 < n_steps:
            pw = each(lambda x: _dot(x, x), pw)
    u_pair = each(lambda x: jnp.where(m0, x[:c], x[c:c2]), u)
    u2 = each(lambda x: _pad_rows(cat(x, x), LANES), u_pair)
    yst = each(lambda x, uu, vv: _dot(x, cat(uu, vv)), m1i, u2, v2)
    y = each(lambda x, y_: x[c:] + jnp.where(m0, y_[:c], y_[c:]), m2, yst)

    p_end = each(lambda x: x[c - 1:c, :], p_incl)
    xt = each(lambda x, y_: _pad_rows(cat(x, y_), LANES), u_pair, v)
    yk = each(lambda x, y_, z: _pad_rows(cat(x * z, y_ * z), LANES), bt, kt, p_end)
    s_new = each(lambda x, z, xx, yy: x * z + _dot(xx.T, yy), s, p_end, xt, yk)
    for p in pairs:
        s_scr[p] = jnp.where(same_head, s_new[p], 0.0)

    mean = each(lambda x: _dot_split(x, e_mean_b), y)
    d = each(lambda x, y_: x - y_, y, mean)
    var = each(lambda x: _dot_split(x * x, e_mean_b), d)
    bonus = each(lambda x, y_, sl: _dot_split(x * y_ * rk_ref[:, sl], e_sum_b), r, k, sls)
    for p in pairs:
        sl = sls[p]
        yn = d[p] * lax.rsqrt(var[p] + GN_EPS) * lnw_ref[:, sl] + lnb_ref[:, sl]
        y_ref[:, sl] = ((yn + bonus[p] * v[p]) * g_ref[:, sl]).astype(y_ref.dtype)

    @pl.when(ci == pl.num_programs(2) - 1)
    def _():
        so_ref[0] = s_scr[...]


def rw_scan(r, lw, k, v, kk, a, g, r_k, ln_w, ln_b, s0, *, nseq, seqlen, chunk, hpb, y_dtype):
    nc = seqlen // chunk
    npair = RW_HEADS // 2
    width = hpb * LANES
    row = lambda b, h, t: (b * nc + t, h)
    par = lambda b, h, t: (0, h)
    st = lambda b, h, t: (b, h, 0, 0)
    kern = functools.partial(_scan_kernel, chunk=chunk, hpb=hpb)
    return pl.pallas_call(
        kern,
        grid=(nseq, npair // hpb, nc),
        in_specs=[pl.BlockSpec((chunk, width), row)] * 7 + [pl.BlockSpec((1, width), par)] * 3
                 + [pl.BlockSpec((1, hpb, LANES, LANES), st)],
        out_specs=[pl.BlockSpec((chunk, width), row), pl.BlockSpec((1, hpb, LANES, LANES), st)],
        out_shape=[jax.ShapeDtypeStruct((nseq * seqlen, RW_WIDTH), y_dtype),
                   jax.ShapeDtypeStruct((nseq, npair, LANES, LANES), F32)],
        scratch_shapes=[pltpu.VMEM((hpb, LANES, LANES), F32)],
        compiler_params=_cparams(("parallel", "parallel", "arbitrary"), 32),
    )(r, lw, k, v, kk, a, g, r_k, ln_w, ln_b, s0)


def _pair_states(s):
    n = s.shape[0]
    s = s.reshape(n, RW_HEADS // 2, 2, RW_HEAD, RW_HEAD)
    z = jnp.zeros_like(s[:, :, 0])
    top = jnp.concatenate([s[:, :, 0], z], axis=-1)
    bot = jnp.concatenate([z, s[:, :, 1]], axis=-1)
    return jnp.concatenate([top, bot], axis=-2)


def _unpair_states(sp):
    n = sp.shape[0]
    s0 = sp[:, :, :RW_HEAD, :RW_HEAD]
    s1 = sp[:, :, RW_HEAD:, RW_HEAD:]
    return jnp.stack([s0, s1], axis=2).reshape(n, RW_HEADS, RW_HEAD, RW_HEAD)


def _kidx_kernel(x_ref, g_ref, kn_ref, kd_ref):
    x = x_ref[...]
    lane = lax.broadcasted_iota(I32, x.shape, 1)
    xk = jnp.where(lane < IDX_DIM, x, 0.0)
    ms = jnp.sum(xk * xk, axis=-1, keepdims=True) * (1.0 / IDX_DIM)
    xd = xk + pltpu.roll(xk, IDX_DIM, 1)
    kd = (xd * lax.rsqrt(ms + RMS_EPS)) * g_ref[...]
    kd_ref[...] = kd
    kn_ref[...] = kd[:, :IDX_DIM]


def kidx_norm(idx_cols, g, *, tm):
    m = idx_cols.shape[0]
    g2 = jnp.concatenate([g, g]).reshape(1, LANES)
    return pl.pallas_call(
        _kidx_kernel,
        grid=(m // tm,),
        in_specs=[pl.BlockSpec((tm, LANES), lambda i: (i, IDX_HEADS * IDX_DIM // LANES)),
                  pl.BlockSpec((1, LANES), lambda i: (0, 0))],
        out_specs=[pl.BlockSpec((tm, IDX_DIM), lambda i: (i, 0)), pl.BlockSpec((tm, LANES), lambda i: (i, 0))],
        out_shape=[jax.ShapeDtypeStruct((m, IDX_DIM), F32), jax.ShapeDtypeStruct((m, LANES), F32)],
        compiler_params=_cparams(("parallel",), 16),
    )(idx_cols, g2)


def _sort_key(score):
    score = jnp.where(score == 0.0, 0.0, score)
    bits = pltpu.bitcast(score, I32)
    return bits ^ (lax.shift_right_arithmetic(bits, 31) & 0x7FFFFFFF)


def _count_lanes(acc):
    return jnp.dot(acc.astype(BF16), jnp.ones((LANES, LANES), BF16), preferred_element_type=F32)


def _topk_select(key_tile, cut_ref, *, rows, n_tiles, topk, col_bits):
    def count(pred):
        acc = jnp.zeros((rows, LANES), F32)
        for t in range(n_tiles):
            acc = acc + jnp.where(pred(key_tile(t), t), 1.0, 0.0)
        return _count_lanes(acc)

    def bit_step(it, thr):
        cand = thr + lax.shift_left(jnp.int32(1), 31 - it)
        cnt = count(lambda kt, t: kt >= cand)
        return jnp.where(cnt >= topk, cand, thr)

    thr = lax.fori_loop(0, 32, bit_step, jnp.full((rows, LANES), INT_MIN, I32))
    n_gt = count(lambda kt, t: kt > thr)
    n_ge = count(lambda kt, t: kt >= thr)
    excess = (n_ge > topk) & (thr > NEG_INF_KEY)
    cut_ref[...] = jnp.full((rows, LANES), BIG_COL, I32)
    lane = lax.broadcasted_iota(I32, (rows, LANES), 1)

    @pl.when(jnp.max(jnp.where(excess, 1.0, 0.0)) > 0.0)
    def _():
        need = topk - n_gt

        def col_step(it, cut):
            cand = cut + lax.shift_left(jnp.int32(1), col_bits - 1 - it)
            cnt = count(lambda kt, t: (kt == thr) & (lane + t * LANES < cand))
            return jnp.where(cnt < need, cand, cut)

        cut = lax.fori_loop(0, col_bits, col_step, jnp.zeros((rows, LANES), I32))
        cut_ref[...] = jnp.where(excess, cut, BIG_COL)

    return thr


def _selected(kt, t, thr, cut):
    lane = lax.broadcasted_iota(I32, kt.shape, 1)
    return ((kt > thr) | ((kt == thr) & (lane + t * LANES <= cut))) & (kt != NEG_INF_KEY)


def _idx_scores(q_ref, wq_ref, kd, *, rows):
    lane = lax.broadcasted_iota(I32, (rows, LANES), 1)
    lo = lane < IDX_DIM
    k_hi, k_lo = _split_bf16(kd)
    k_cat = jnp.concatenate([k_hi, k_lo], axis=1)
    zero = jnp.zeros((rows, LANES), BF16)
    score = jnp.zeros((rows, kd.shape[0]), F32)
    for hp in range(IDX_HEADS // 2):
        qp = q_ref[:, hp * LANES:(hp + 1) * LANES]
        q_hi = qp.astype(BF16)
        q_lo = pltpu.roll(qp - q_hi.astype(F32), IDX_DIM, 1).astype(BF16)
        for half in range(2):
            h = 2 * hp + half
            own = lo if half == 0 else jnp.logical_not(lo)
            lhs = jnp.concatenate([jnp.where(own, q_hi, q_lo), jnp.where(own, q_hi, zero)], axis=1)
            d = lax.dot_general(lhs, k_cat, (((1,), (1,)), ((), ())), preferred_element_type=F32)
            wh = wq_ref[:, IDX_DIM + h:IDX_DIM + h + 1] * IDX_SCALE
            score = score + wh * jnp.maximum(d, 0.0)
    return score


def _dot_nt3(a, b_hi, b_lo):
    a_hi, a_lo = _split_bf16(a)
    nt = lambda x, y: lax.dot_general(x, y, (((1,), (1,)), ((), ())), preferred_element_type=F32)
    return nt(a_hi, b_hi) + (nt(a_hi, b_lo) + nt(a_lo, b_hi))


def _prompt_index_kernel(q_ref, wq_ref, kd_ref, m_ref, key_scr, cut_scr, *, tq, q0, kv, n_tiles, topk):
    cols = n_tiles * LANES
    q_pos = q0 + lax.broadcasted_iota(I32, (tq, cols), 0)
    score = _idx_scores(q_ref, wq_ref, kd_ref[0], rows=tq)
    col = lax.broadcasted_iota(I32, (tq, cols), 1)
    key_scr[...] = _sort_key(jnp.where(col <= q_pos, score, -jnp.inf))
    key_tile = lambda t: key_scr[:, t * LANES:(t + 1) * LANES]
    thr = _topk_select(key_tile, cut_scr, rows=tq, n_tiles=n_tiles, topk=topk, col_bits=12)
    cut = cut_scr[...]
    for t in range(n_tiles):
        width = min(LANES, kv - t * LANES)
        sel = jnp.where(_selected(key_tile(t), t, thr, cut), 1.0, 0.0).astype(m_ref.dtype)
        m_ref[:, t * LANES:t * LANES + width] = sel[:, :width]


def prompt_index_mask(idx_cols, kdup_pad, *, nseq, seqlen, tq, qi, topk):
    nq = seqlen // tq
    kv = (qi + 1) * tq
    n_tiles = -(-kv // LANES)
    cols = n_tiles * LANES
    assert cols <= kdup_pad.shape[1]
    kern = functools.partial(_prompt_index_kernel, tq=tq, q0=qi * tq, kv=kv, n_tiles=n_tiles, topk=topk)
    return pl.pallas_call(
        kern,
        grid=(nseq,),
        in_specs=[pl.BlockSpec((tq, IDX_HEADS * IDX_DIM), lambda b: (b * nq + qi, 0)),
                  pl.BlockSpec((tq, LANES), lambda b: (b * nq + qi, IDX_HEADS * IDX_DIM // LANES)),
                  pl.BlockSpec((1, cols, LANES), lambda b: (b, 0, 0))],
        out_specs=pl.BlockSpec((tq, kv), lambda b: (b, 0)),
        out_shape=jax.ShapeDtypeStruct((nseq * tq, kv), BF16),
        scratch_shapes=[pltpu.VMEM((tq, cols), I32), pltpu.VMEM((tq, LANES), I32)],
        compiler_params=_cparams(("parallel",)),
    )(idx_cols, idx_cols, kdup_pad)


def _prompt_attn_kernel(q_ref, k_ref, v_ref, m_ref, o_ref):
    logits = _dot_nt(q_ref[...] * ATT_SCALE, k_ref[0])
    logits = jnp.where(m_ref[...] > 0, logits, -jnp.inf)
    mx = jnp.max(logits, axis=-1, keepdims=True)
    e = jnp.exp(logits - mx)
    inv = 1.0 / jnp.sum(e, axis=-1, keepdims=True)
    o_ref[0] = (_dot(e, v_ref[0]) * inv).astype(o_ref.dtype)


def prompt_attention(qkv, mask, *, nseq, seqlen, tq, qi):
    nq = seqlen // tq
    kv = (qi + 1) * tq
    qkv3 = qkv.reshape(nseq, seqlen, 3 * ATT_WIDTH)
    return pl.pallas_call(
        _prompt_attn_kernel,
        grid=(nseq, ATT_HEADS),
        in_specs=[pl.BlockSpec((tq, ATT_HEAD), lambda b, h: (b * nq + qi, h)),
                  pl.BlockSpec((1, kv, ATT_HEAD), lambda b, h: (b, 0, ATT_HEADS + h)),
                  pl.BlockSpec((1, kv, ATT_HEAD), lambda b, h: (b, 0, 2 * ATT_HEADS + h)),
                  pl.BlockSpec((tq, kv), lambda b, h: (b, 0))],
        out_specs=pl.BlockSpec((1, tq, ATT_HEAD), lambda b, h: (b, 0, h)),
        out_shape=jax.ShapeDtypeStruct((nseq, tq, ATT_WIDTH), BF16),
        compiler_params=_cparams(("parallel", "arbitrary")),
    )(qkv, qkv3, qkv3, mask)


PAGES_PER_STEP = 8


def _sample_score_kernel(pt_ref, q_ref, w_ref, *rest, dsq):
    page_refs, o_ref = rest[:-1], rest[-1]
    q = q_ref[0]
    w = w_ref[0]
    for j, page_ref in enumerate(page_refs):
        p_hi, p_lo = _split_bf16(page_ref[0, 0])
        d = jnp.maximum(_dot_nt3(q, p_hi, p_lo), 0.0) * w
        o_ref[0, :, j * PAGE_SIZE:(j + 1) * PAGE_SIZE] = jnp.sum(d.reshape(IDX_HEADS, dsq, PAGE_SIZE), axis=0)


def sample_page_scores(page_table, qh, wh, cache_kidx, layer, *, nb, dsq, n_pages):
    kern = functools.partial(_sample_score_kernel, dsq=dsq)
    rows = IDX_HEADS * dsq
    pps = PAGES_PER_STEP
    assert n_pages % pps == 0

    def page_spec(j):
        return pl.BlockSpec((1, 1, PAGE_SIZE, IDX_DIM),
                            lambda b, p, pt: (layer, pt[b * n_pages + p * pps + j], 0, 0))

    grid_spec = pltpu.PrefetchScalarGridSpec(
        num_scalar_prefetch=1,
        grid=(nb, n_pages // pps),
        in_specs=[pl.BlockSpec((1, rows, IDX_DIM), lambda b, p, pt: (b, 0, 0)),
                  pl.BlockSpec((1, rows, 1), lambda b, p, pt: (b, 0, 0))] + [page_spec(j) for j in range(pps)],
        out_specs=pl.BlockSpec((1, dsq, pps * PAGE_SIZE), lambda b, p, pt: (b, 0, p)),
    )
    return pl.pallas_call(
        kern,
        grid_spec=grid_spec,
        out_shape=jax.ShapeDtypeStruct((nb, dsq, n_pages * PAGE_SIZE), F32),
        compiler_params=_cparams(("parallel", "arbitrary"), 16),
    )(page_table.reshape(-1), qh, wh, *([cache_kidx] * pps))


def _sample_select_kernel(sc_ref, q_ref, w_ref, kn_ref, idx_ref, key_scr, keyq_scr, rank_scr, cut_scr,
                          *, dsq, past, topk, nt_pad):
    n_past_tiles = past // LANES
    n_tiles = n_past_tiles + 1
    keyq_scr[...] = jnp.full(keyq_scr.shape, NEG_INF_KEY, I32)

    def put_tile(t, keys):
        key_scr[t] = keys
        for q in range(dsq):
            keyq_scr[q, t:t + 1, :] = keys[q:q + 1, :]

    for t in range(n_past_tiles):
        put_tile(t, _sort_key(sc_ref[0, :, t * LANES:(t + 1) * LANES]))
    kn_hi, kn_lo = _split_bf16(kn_ref[0])
    d = jnp.maximum(_dot_nt3(q_ref[0], kn_hi, kn_lo), 0.0) * w_ref[0]
    s_new = jnp.sum(d.reshape(IDX_HEADS, dsq, LANES), axis=0)
    qrow = lax.broadcasted_iota(I32, (dsq, LANES), 0)
    jcol = lax.broadcasted_iota(I32, (dsq, LANES), 1)
    put_tile(n_past_tiles, _sort_key(jnp.where(jcol <= qrow, s_new, -jnp.inf)))
    thr = _topk_select(lambda t: key_scr[t], cut_scr, rows=dsq, n_tiles=n_tiles, topk=topk, col_bits=15)
    cut = cut_scr[...]

    li = lax.broadcasted_iota(I32, (LANES, LANES), 0)
    lj = lax.broadcasted_iota(I32, (LANES, LANES), 1)
    upper = jnp.where(li < lj, 1.0, 0.0).astype(BF16)
    ones = jnp.ones((LANES, LANES), BF16)
    ti = lax.broadcasted_iota(I32, (nt_pad, nt_pad), 0)
    tj = lax.broadcasted_iota(I32, (nt_pad, nt_pad), 1)
    earlier = jnp.where(tj < ti, 1.0, 0.0).astype(BF16)
    col = (lax.broadcasted_iota(I32, (nt_pad, LANES), 0) * LANES + lax.broadcasted_iota(I32, (nt_pad, LANES), 1))
    slot_f = lax.broadcasted_iota(I32, (topk, LANES), 0).astype(F32)
    lane_f = lax.broadcasted_iota(I32, (topk, LANES), 1).astype(F32)
    for q in range(dsq):
        kq = keyq_scr[q]
        thr_q = thr[q:q + 1]
        cut_q = cut[q:q + 1]
        sel = ((kq > thr_q) | ((kq == thr_q) & (col <= cut_q))) & (kq != NEG_INF_KEY)
        selb = jnp.where(sel, 1.0, 0.0).astype(BF16)
        within = jnp.dot(selb, upper, preferred_element_type=F32)
        totals = jnp.dot(selb, ones, preferred_element_type=F32)
        before = jnp.dot(earlier, totals.astype(BF16), preferred_element_type=F32)
        rank_scr[...] = jnp.where(sel, before + within, -1.0)

        def tile_step(t, acc):
            rk = jnp.broadcast_to(rank_scr[pl.ds(t, 1), :], (topk, LANES))
            return acc + jnp.where(rk == slot_f, lane_f + jnp.asarray(t * LANES).astype(F32), 0.0)

        acc = lax.fori_loop(0, n_tiles, tile_step, jnp.zeros((topk, LANES), F32))
        idx_ref[0, q] = jnp.sum(acc, axis=1, keepdims=True).astype(I32)


def sample_select(scores, qh, wh, kn_pad, *, nb, dsq, past, topk):
    rows = IDX_HEADS * dsq
    n_tiles = past // LANES + 1
    nt_pad = -(-n_tiles // LANES) * LANES
    kern = functools.partial(_sample_select_kernel, dsq=dsq, past=past, topk=topk, nt_pad=nt_pad)
    return pl.pallas_call(
        kern,
        grid=(nb,),
        in_specs=[pl.BlockSpec((1, dsq, past), lambda b: (b, 0, 0)),
                  pl.BlockSpec((1, rows, IDX_DIM), lambda b: (b, 0, 0)),
                  pl.BlockSpec((1, rows, 1), lambda b: (b, 0, 0)),
                  pl.BlockSpec((1, LANES, IDX_DIM), lambda b: (b, 0, 0))],
        out_specs=pl.BlockSpec((1, dsq, topk, 1), lambda b: (b, 0, 0, 0)),
        out_shape=jax.ShapeDtypeStruct((nb, dsq, topk, 1), I32),
        scratch_shapes=[pltpu.VMEM((n_tiles, dsq, LANES), I32), pltpu.VMEM((dsq, nt_pad, LANES), I32),
                        pltpu.VMEM((nt_pad, LANES), F32), pltpu.VMEM((dsq, LANES), I32)],
        compiler_params=_cparams(("parallel",), 32),
    )(scores, qh, wh, kn_pad)


GATHER_UNROLL = 8


def _sample_attn_kernel(idx_ref, pt_ref, q_ref, pos_ref, kn_ref, vn_ref, ck_ref, cv_ref, o_ref,
                        kbuf, vbuf, sem, *, layer, dsq, past, n_pages, topk):
    b = pl.program_id(0)

    def start_query(q, slot):
        def body(j8, pos_max):
            for jj in range(GATHER_UNROLL):
                j = j8 * GATHER_UNROLL + jj
                pos = idx_ref[(b * dsq + q) * topk + j]
                ppos = jnp.minimum(pos, past - 1)
                page = pt_ref[b * n_pages + lax.shift_right_logical(ppos, 7)]
                off = ppos & (PAGE_SIZE - 1)
                pltpu.make_async_copy(ck_ref.at[layer, page, off], kbuf.at[slot, :, j], sem.at[0, slot]).start()
                pltpu.make_async_copy(cv_ref.at[layer, page, off], vbuf.at[slot, :, j], sem.at[1, slot]).start()
                pos_max = jnp.maximum(pos_max, pos)
            return pos_max

        return lax.fori_loop(0, topk // GATHER_UNROLL, body, jnp.int32(0))

    def wait_query(slot):
        pltpu.make_async_copy(kbuf.at[1 - slot], kbuf.at[slot], sem.at[0, slot]).wait()
        pltpu.make_async_copy(vbuf.at[1 - slot], vbuf.at[slot], sem.at[1, slot]).wait()

    def patch_new_rows(q, slot):
        rel = pos_ref[0, q] - past
        lane = lax.broadcasted_iota(I32, (topk, LANES), 1)
        onehot = jnp.where(rel == lane, 1.0, 0.0).astype(BF16)
        is_new = rel >= 0
        pad = jnp.zeros((LANES - dsq, ATT_HEAD), F32)
        for h in range(ATT_HEADS):
            for buf, new_ref in ((kbuf, kn_ref), (vbuf, vn_ref)):
                rows = jnp.concatenate([new_ref[0, :, h, :], pad], axis=0)
                hi, lo = _split_bf16(rows)
                fix = (jnp.dot(onehot, hi, preferred_element_type=F32) + jnp.dot(onehot, lo, preferred_element_type=F32))
                buf[slot, h] = jnp.where(is_new, fix, buf[slot, h])

    pos_max = start_query(0, 0)
    for q in range(dsq):
        slot = q % 2
        next_max = start_query(q + 1, 1 - slot) if q + 1 < dsq else None
        wait_query(slot)

        @pl.when(pos_max >= past)
        def _():
            patch_new_rows(q, slot)

        heads = range(ATT_HEADS)
        qs = [jnp.broadcast_to(q_ref[0, q, h:h + 1, :], (8, ATT_HEAD)) for h in heads]
        logits = jnp.concatenate([_dot_nt(qs[h], kbuf[slot, h])[0:1] for h in heads], axis=0) * ATT_SCALE
        mx = jnp.max(logits, axis=-1, keepdims=True)
        e = jnp.exp(logits - mx)
        prob = e / jnp.sum(e, axis=-1, keepdims=True)
        outs = [_dot(jnp.broadcast_to(prob[h:h + 1], (8, topk)), vbuf[slot, h])[0:1] for h in heads]
        o_ref[0, q:q + 1, :] = jnp.concatenate(outs, axis=1)
        pos_max = next_max


def sample_attention(sel_idx, page_table, q, k_new, v_new, cache_k, cache_v, layer, *, nb, dsq, past, topk):
    n_pages = past // PAGE_SIZE
    assert topk % GATHER_UNROLL == 0
    kern = functools.partial(_sample_attn_kernel, layer=layer, dsq=dsq, past=past, n_pages=n_pages, topk=topk)
    new_spec = pl.BlockSpec((1, dsq, ATT_HEADS, ATT_HEAD), lambda b, idx, pt: (b, 0, 0, 0))
    grid_spec = pltpu.PrefetchScalarGridSpec(
        num_scalar_prefetch=2,
        grid=(nb,),
        in_specs=[new_spec,
                  pl.BlockSpec((1, dsq, topk, 1), lambda b, idx, pt: (b, 0, 0, 0)),
                  new_spec, new_spec,
                  pl.BlockSpec(memory_space=pl.ANY), pl.BlockSpec(memory_space=pl.ANY)],
        out_specs=pl.BlockSpec((1, dsq, ATT_WIDTH), lambda b, idx, pt: (b, 0, 0)),
        scratch_shapes=[pltpu.VMEM((2, ATT_HEADS, topk, ATT_HEAD), F32),
                        pltpu.VMEM((2, ATT_HEADS, topk, ATT_HEAD), F32),
                        pltpu.SemaphoreType.DMA((2, 2))],
    )
    return pl.pallas_call(
        kern,
        grid_spec=grid_spec,
        out_shape=jax.ShapeDtypeStruct((nb, dsq, ATT_WIDTH), F32),
        compiler_params=_cparams(("arbitrary",), 32),
    )(sel_idx.reshape(-1), page_table.reshape(-1), q, sel_idx, k_new, v_new, cache_k, cache_v)


def _conv_gate_kernel(ug_ref, uv_ref, pg_ref, pv_ref, wg_ref, wv_ref, bg_ref, bv_ref, o_ref, *, nb, seqlen):
    def conv(u, prev, w, bias):
        return w[0:1, :] * _shift_rows(u, prev, 2) + w[1:2, :] * _shift_rows(u, prev[1:2, :], 1) + w[2:3, :] * u + bias

    outs = []
    for s in range(nb):
        rows = slice(s * seqlen, (s + 1) * seqlen)
        gate = conv(ug_ref[rows, :], pg_ref[s], wg_ref[...], bg_ref[...])
        val = conv(uv_ref[rows, :], pv_ref[s], wv_ref[...], bv_ref[...])
        outs.append(gate * jax.nn.sigmoid(gate) * val)
    o_ref[...] = (outs[0] if nb == 1 else jnp.concatenate(outs, axis=0)).astype(o_ref.dtype)


def conv_gate(u, conv_prev, conv_w, conv_b, *, nseq, seqlen, nb, tc):
    m = nseq * seqlen
    half = D_FF // tc
    kern = functools.partial(_conv_gate_kernel, nb=nb, seqlen=seqlen)
    rows = nb * seqlen
    return pl.pallas_call(
        kern,
        grid=(nseq // nb, half),
        in_specs=[pl.BlockSpec((rows, tc), lambda b, j: (b, j)),
                  pl.BlockSpec((rows, tc), lambda b, j: (b, j + half)),
                  pl.BlockSpec((nb, CONV_W - 1, tc), lambda b, j: (b, 0, j)),
                  pl.BlockSpec((nb, CONV_W - 1, tc), lambda b, j: (b, 0, j + half)),
                  pl.BlockSpec((CONV_W, tc), lambda b, j: (0, j)),
                  pl.BlockSpec((CONV_W, tc), lambda b, j: (0, j + half)),
                  pl.BlockSpec((1, tc), lambda b, j: (0, j)),
                  pl.BlockSpec((1, tc), lambda b, j: (0, j + half))],
        out_specs=pl.BlockSpec((rows, tc), lambda b, j: (b, j)),
        out_shape=jax.ShapeDtypeStruct((m, D_FF), BF16),
        compiler_params=_cparams(("parallel", "arbitrary"), 48),
    )(u, u, conv_prev, conv_prev, conv_w, conv_w, conv_b, conv_b)


QKV_COL0 = RW_PAD
IDX_COL0 = QKV_COL0 + 3 * ATT_WIDTH
GATE_COL0 = IDX_COL0 + IDX_PAD
W_IN_PAD = GATE_COL0 + 2 * D_MODEL


def _realign_w_in(w_in):
    a0 = RW_COLS
    i0 = a0 + 3 * ATT_WIDTH
    g0 = a0 + ATT_COLS
    zeros = lambda n: jnp.zeros(w_in.shape[:2] + (n,), w_in.dtype)
    parts = [w_in[..., :a0], zeros(RW_PAD - RW_COLS), w_in[..., a0:i0], w_in[..., i0:g0],
             zeros(IDX_PAD - (g0 - i0)), w_in[..., g0:]]
    return jnp.concatenate(parts, axis=-1).astype(BF16)


def _layer_weights(l, rw_mu, rw_g2):
    mu = jnp.pad(rw_mu[l], (0, RW_PAD - RW_COLS)).reshape(1, RW_PAD)
    g2 = jnp.pad(rw_g2[l], ((0, G_PAD - G_LORA), (0, 0)))
    return mu, g2


def _trunk_layer(x, l, p, lw, shift_prev, wkv_prev, conv_prev, attend, *, nseq, seqlen, tiles):
    m = nseq * seqlen
    mu, g2 = lw
    w_all = p['w_in_bf16']
    row = lambda a: a.reshape(1, -1)
    h = rmsnorm_rows(x, p['norm_mix'][l], tm=tiles['rms'], out_dtype=BF16)
    proj = lambda col0, n, tn: matmul(h, w_all, layer=l, col0=col0, n_cols=n, tm=tiles['tm'], tn=tn)
    rw = proj(0, RW_PAD, tiles['tn_rw'])
    qkv = proj(QKV_COL0, 3 * ATT_WIDTH, tiles['tn_qkv'])
    idx_cols = proj(IDX_COL0, IDX_PAD, tiles['tn_idx'])
    gates = proj(GATE_COL0, 2 * D_MODEL, tiles['tn_gate'])

    sp = jnp.pad(shift_prev, ((0, 0), (0, 0), (0, RW_PAD - RW_COLS)))
    r, lwd, k2, v, kk, a, g = rw_prep(rw, sp, mu, row(p['rw_w0'][l]), p['rw_w2'][l], row(p['rw_a0'][l]),
                                       p['rw_a2'][l], g2, row(p['rw_k_k'][l]), row(p['rw_k_a'][l]),
                                       nseq=nseq, seqlen=seqlen, tt=tiles['chunk'])
    y_a, s_pair = rw_scan(r, lwd, k2, v, kk, a, g, row(p['rw_r_k'][l]), row(p['rw_ln_w'][l]), row(p['rw_ln_b'][l]),
                          _pair_states(wkv_prev), nseq=nseq, seqlen=seqlen, chunk=tiles['chunk'], hpb=tiles['hpb'],
                          y_dtype=tiles['y_dtype'])
    y_a = y_a.astype(BF16)
    wkv_new = _unpair_states(s_pair)
    shift_new = rw.reshape(nseq, seqlen, RW_PAD)[:, seqlen - 1:, :RW_COLS]

    ki_n, ki_dup = kidx_norm(idx_cols, p['idx_k_norm'][l], tm=tiles['kidx'])
    y_b = attend(qkv, idx_cols, ki_n, ki_dup)

    mix = branch_mix(y_a, y_b, p['w_branch_a'], p['w_branch_b'], gates, tm=tiles['tm'], tn=tiles['tn_mix'], layer=l)
    x = matmul(mix, p['w_out'], layer=l, tm=tiles['tm'], tn=tiles['tn_out'], residual=x)

    h2 = rmsnorm_rows(x, p['norm_ffn'][l], tm=tiles['rms'], out_dtype=BF16)
    u = matmul(h2, p['w_up'], layer=l, tm=tiles['tm'], tn=tiles['tn_up'])
    act = conv_gate(u, conv_prev, p['conv_w'][l], row(p['conv_b'][l]), nseq=nseq, seqlen=seqlen,
                    nb=tiles['conv_nb'], tc=tiles['conv_tc'])
    u3 = u.reshape(nseq, seqlen, 2 * D_FF)
    conv_new = jnp.concatenate([conv_prev, u3], axis=1)[:, seqlen:] if seqlen < CONV_W - 1 else u3[:, seqlen - (CONV_W - 1):]
    x = matmul(act, p['w_down'], layer=l, tm=tiles['tm_down'], tn=tiles['tn_down'], residual=x)

    k_new = qkv[:, ATT_WIDTH:2 * ATT_WIDTH].reshape(nseq, seqlen, ATT_HEADS, ATT_HEAD)
    v_new = qkv[:, 2 * ATT_WIDTH:].reshape(nseq, seqlen, ATT_HEADS, ATT_HEAD)
    return x, (k_new, v_new, ki_n.reshape(nseq, seqlen, IDX_DIM), wkv_new, shift_new, conv_new)


PROMPT_TILES = dict(rms=688, tm=1376, tn_rw=768, tn_qkv=768, tn_idx=256, tn_gate=512, tn_mix=256, tn_out=256, tn_up=512,
                    tm_down=688, tn_down=256, chunk=48, hpb=8, kidx=688, conv_nb=1, conv_tc=256, tq=688, y_dtype=BF16)
SAMPLE_TILES = dict(rms=64, tm=64, tn_rw=1152, tn_qkv=768, tn_idx=256, tn_gate=1024, tn_mix=1024, tn_out=1024, tn_up=512,
                    tm_down=64, tn_down=512, chunk=8, hpb=4, kidx=64, conv_nb=8, conv_tc=256, y_dtype=F32)


def kernel(x_prompt, x_sample, cache_k, cache_v, cache_kidx, state_wkv, state_shift, state_conv, page_table,
           meta_tokens, norm_mix, w_in, rw_mu, rw_w0, rw_w2, rw_a0, rw_a2, rw_g2, rw_k_k, rw_k_a, rw_r_k,
           rw_ln_w, rw_ln_b, idx_k_norm, w_branch_a, w_branch_b, w_out, norm_ffn, w_up, conv_w, conv_b,
           w_down, norm_final):
    p = {'norm_mix': norm_mix, 'rw_w0': rw_w0, 'rw_w2': rw_w2, 'rw_a0': rw_a0, 'rw_a2': rw_a2,
         'rw_k_k': rw_k_k, 'rw_k_a': rw_k_a, 'rw_r_k': rw_r_k.reshape(rw_r_k.shape[0], RW_WIDTH),
         'rw_ln_w': rw_ln_w, 'rw_ln_b': rw_ln_b, 'idx_k_norm': idx_k_norm, 'w_branch_a': w_branch_a,
         'w_branch_b': w_branch_b, 'w_out': w_out, 'norm_ffn': norm_ffn, 'w_up': w_up, 'conv_w': conv_w,
         'conv_b': conv_b, 'w_down': w_down, 'w_in_bf16': _realign_w_in(w_in)}
    depth = w_in.shape[0]
    nb_p, seq = x_prompt.shape[:2]
    tp = seq + N_META
    nb_s, ds = x_sample.shape[:2]
    n_pages = page_table.shape[1]
    past = n_pages * PAGE_SIZE

    meta = jnp.broadcast_to(meta_tokens[None], (nb_p, N_META, D_MODEL))
    xp = jnp.concatenate([meta, x_prompt], axis=1).reshape(nb_p * tp, D_MODEL)
    xs = x_sample.reshape(nb_s * ds, D_MODEL)

    topk_p = min(TOPK_MAX, tp // 4)
    topk_s = min(TOPK_MAX, (past + ds) // 4)
    lp = -(-tp // LANES) * LANES
    tq = PROMPT_TILES['tq']

    def prompt_attend(qkv, idx_cols, ki_n, ki_dup):
        kd = jnp.pad(ki_dup.reshape(nb_p, tp, LANES), ((0, 0), (0, lp - tp), (0, 0)))
        outs = []
        for qi in range(tp // tq):
            mask = prompt_index_mask(idx_cols, kd, nseq=nb_p, seqlen=tp, tq=tq, qi=qi, topk=topk_p)
            outs.append(prompt_attention(qkv, mask, nseq=nb_p, seqlen=tp, tq=tq, qi=qi))
        return jnp.concatenate(outs, axis=1).reshape(nb_p * tp, ATT_WIDTH)

    def make_sample_attend(l):
        def attend(qkv, idx_cols, ki_n, ki_dup):
            qi = idx_cols[:, :IDX_HEADS * IDX_DIM].reshape(nb_s, ds, IDX_HEADS, IDX_DIM)
            qh = jnp.swapaxes(qi, 1, 2).reshape(nb_s, IDX_HEADS * ds, IDX_DIM)
            wi = idx_cols[:, IDX_HEADS * IDX_DIM + IDX_DIM:IDX_HEADS * IDX_DIM + IDX_DIM + IDX_HEADS]
            wh = jnp.swapaxes(wi.reshape(nb_s, ds, IDX_HEADS), 1, 2).reshape(nb_s, IDX_HEADS * ds, 1) * IDX_SCALE
            scores = sample_page_scores(page_table, qh, wh, cache_kidx, l, nb=nb_s, dsq=ds, n_pages=n_pages)
            kn_pad = jnp.pad(ki_n.reshape(nb_s, ds, IDX_DIM), ((0, 0), (0, LANES - ds), (0, 0)))
            sel = sample_select(scores, qh, wh, kn_pad, nb=nb_s, dsq=ds, past=past, topk=topk_s)
            q = qkv[:, :ATT_WIDTH].reshape(nb_s, ds, ATT_HEADS, ATT_HEAD)
            k_new = qkv[:, ATT_WIDTH:2 * ATT_WIDTH].reshape(nb_s, ds, ATT_HEADS, ATT_HEAD)
            v_new = qkv[:, 2 * ATT_WIDTH:].reshape(nb_s, ds, ATT_HEADS, ATT_HEAD)
            y = sample_attention(sel, page_table, q, k_new, v_new, cache_k, cache_v, l,
                                 nb=nb_s, dsq=ds, past=past, topk=topk_s)
            return y.reshape(nb_s * ds, ATT_WIDTH).astype(BF16)
        return attend

    zero_shift = jnp.zeros((nb_p, 1, RW_COLS), F32)
    zero_wkv = jnp.zeros((nb_p, RW_HEADS, RW_HEAD, RW_HEAD), F32)
    zero_conv = jnp.zeros((nb_p, CONV_W - 1, 2 * D_FF), F32)
    sts_p, sts_s = [], []
    for l in range(depth):
        lw = _layer_weights(l, rw_mu, rw_g2)
        xp, st = _trunk_layer(xp, l, p, lw, zero_shift, zero_wkv, zero_conv, prompt_attend,
                              nseq=nb_p, seqlen=tp, tiles=PROMPT_TILES)
        sts_p.append(st)
        xs, st = _trunk_layer(xs, l, p, lw, state_shift[l], state_wkv[l], state_conv[l], make_sample_attend(l),
                              nseq=nb_s, seqlen=ds, tiles=SAMPLE_TILES)
        sts_s.append(st)

    y_prompt = rmsnorm_rows(xp, norm_final, tm=PROMPT_TILES['rms'], out_dtype=F32)
    y_prompt = y_prompt.reshape(nb_p, tp, D_MODEL)[:, N_META:]
    y_sample = rmsnorm_rows(xs, norm_final, tm=SAMPLE_TILES['rms'], out_dtype=F32).reshape(nb_s, ds, D_MODEL)
    stk = lambda sts, i: jnp.stack([s[i] for s in sts])
    return (y_prompt, y_sample,
            stk(sts_p, 0), stk(sts_p, 1), stk(sts_p, 2), stk(sts_p, 3), stk(sts_p, 4), stk(sts_p, 5),
            stk(sts_s, 0), stk(sts_s, 1), stk(sts_s, 2), stk(sts_s, 3), stk(sts_s, 4), stk(sts_s, 5))
```

```python
import functools

import jax
import jax.numpy as jnp
from jax import lax
from jax.experimental import pallas as pl
from jax.experimental.pallas import tpu as pltpu

F32 = jnp.float32
BF16 = jnp.bfloat16
I32 = jnp.int32
HIGHEST = lax.Precision.HIGHEST

D_MODEL = 4096
N_META = 16
PAGE_SIZE = 128
RW_HEAD = 64
RW_HEADS = 32
RW_WIDTH = RW_HEADS * RW_HEAD
W_LORA = 128
A_LORA = 128
G_LORA = 480
RW_COLS = 3 * RW_WIDTH + W_LORA + A_LORA + G_LORA
ATT_HEADS = 16
ATT_HEAD = 128
ATT_WIDTH = ATT_HEADS * ATT_HEAD
IDX_HEADS = 16
IDX_DIM = 64
TOPK_MAX = 256
ATT_COLS = 3 * ATT_WIDTH + IDX_HEADS * IDX_DIM + IDX_DIM + IDX_HEADS
D_FF = 11008
CONV_W = 3
RMS_EPS = 1e-6
GN_EPS = 64e-5
ATT_SCALE = ATT_HEAD ** -0.5
IDX_SCALE = (IDX_HEADS ** -0.5) * (IDX_DIM ** -0.5)

LANES = 128
RW_PAD = 6912
IDX_PAD = 1280
G_PAD = 512
INT_MIN = -2 ** 31
NEG_INF_KEY = INT_MIN + 0x7FFFFF
BIG_COL = 2 ** 30
VMEM_MB = 60


def _cparams(sem, mb=VMEM_MB):
    return pltpu.CompilerParams(dimension_semantics=sem, vmem_limit_bytes=mb * 1024 * 1024)


def _dot(a, b):
    return jnp.dot(a.astype(BF16), b.astype(BF16), preferred_element_type=F32)


def _dot_nt(a, b, precision=None):
    if precision is None:
        a, b = a.astype(BF16), b.astype(BF16)
    return lax.dot_general(a, b, (((1,), (1,)), ((), ())), precision=precision,
                           preferred_element_type=F32)


def _dot_hi(a, b):
    return jnp.dot(a, b, precision=HIGHEST, preferred_element_type=F32)


def _split_bf16(x):
    hi = x.astype(BF16)
    return hi, (x - hi.astype(F32)).astype(BF16)


def _dot_split(x, m2_bf16):
    hi, lo = _split_bf16(x)
    return jnp.dot(jnp.concatenate([hi, lo], axis=1), m2_bf16, preferred_element_type=F32)


def _dot_split_rhs(m_bf16, x):
    hi, lo = _split_bf16(x)
    out = jnp.dot(m_bf16, jnp.concatenate([hi, lo], axis=1), preferred_element_type=F32)
    return out[:, :LANES] + out[:, LANES:]


def _rms_kernel(x_ref, g_ref, o_ref):
    x = x_ref[...]
    ms = jnp.sum(x * x, axis=-1, keepdims=True) * (1.0 / x.shape[-1])
    o_ref[...] = ((x * lax.rsqrt(ms + RMS_EPS)) * g_ref[...]).astype(o_ref.dtype)


def rmsnorm_rows(x, g, *, tm, out_dtype):
    m, d = x.shape
    return pl.pallas_call(
        _rms_kernel,
        grid=(m // tm,),
        in_specs=[pl.BlockSpec((tm, d), lambda i: (i, 0)), pl.BlockSpec((1, d), lambda i: (0, 0))],
        out_specs=pl.BlockSpec((tm, d), lambda i: (i, 0)),
        out_shape=jax.ShapeDtypeStruct((m, d), out_dtype),
        compiler_params=_cparams(("parallel",)),
    )(x, g.reshape(1, d))


def _mm_kernel(x_ref, w_ref, o_ref):
    o_ref[...] = _dot(x_ref[...], w_ref[0]).astype(o_ref.dtype)


def _mm_res_kernel(x_ref, w_ref, r_ref, o_ref):
    o_ref[...] = (r_ref[...] + _dot(x_ref[...], w_ref[0])).astype(o_ref.dtype)


def _layer_spec(k, tn, layer, col_blk0=0):
    return pl.BlockSpec((1, k, tn), lambda i, j: (layer, 0, col_blk0 + j))


def matmul(x, w, *, tm, tn, layer=0, col0=0, n_cols=None, out_dtype=F32, residual=None):
    m, k = x.shape
    n = w.shape[2] if n_cols is None else n_cols
    assert m % tm == 0 and n % tn == 0 and col0 % tn == 0, (m, tm, n, tn, col0)
    in_specs = [pl.BlockSpec((tm, k), lambda i, j: (i, 0)), _layer_spec(k, tn, layer, col0 // tn)]
    args = [x, w]
    kern = _mm_kernel
    if residual is not None:
        in_specs.append(pl.BlockSpec((tm, tn), lambda i, j: (i, j)))
        args.append(residual)
        kern = _mm_res_kernel
    return pl.pallas_call(
        kern,
        grid=(m // tm, n // tn),
        in_specs=in_specs,
        out_specs=pl.BlockSpec((tm, tn), lambda i, j: (i, j)),
        out_shape=jax.ShapeDtypeStruct((m, n), out_dtype),
        compiler_params=_cparams(("parallel", "arbitrary")),
    )(*args)


def _mix_kernel(ya_ref, yb_ref, wa_ref, wb_ref, ga_ref, gb_ref, o_ref):
    pa = _dot(ya_ref[...], wa_ref[0])
    pb = _dot(yb_ref[...], wb_ref[0])
    o_ref[...] = (jax.nn.sigmoid(ga_ref[...]) * pa + jax.nn.sigmoid(gb_ref[...]) * pb).astype(o_ref.dtype)


def branch_mix(ya, yb, wa, wb, gates, *, tm, tn, layer):
    m, k = ya.shape
    n = wa.shape[2]
    assert m % tm == 0 and n % tn == 0, (m, tm, n, tn)
    nb = n // tn
    return pl.pallas_call(
        _mix_kernel,
        grid=(m // tm, nb),
        in_specs=[pl.BlockSpec((tm, k), lambda i, j: (i, 0)),
                  pl.BlockSpec((tm, k), lambda i, j: (i, 0)),
                  _layer_spec(k, tn, layer),
                  _layer_spec(k, tn, layer),
                  pl.BlockSpec((tm, tn), lambda i, j: (i, j)),
                  pl.BlockSpec((tm, tn), lambda i, j: (i, j + nb))],
        out_specs=pl.BlockSpec((tm, tn), lambda i, j: (i, j)),
        out_shape=jax.ShapeDtypeStruct((m, n), BF16),
        compiler_params=_cparams(("parallel", "arbitrary")),
    )(ya, yb, wa, wb, gates, gates)


def _shift_rows(x, first_rows, n):
    rolled = pltpu.roll(x, n, 0)
    row = lax.broadcasted_iota(I32, x.shape, 0)
    out = rolled
    for j in range(n):
        out = jnp.where(row == j, first_rows[j:j + 1, :], out)
    return out


def _rw_prep_kernel(x_ref, sp_ref, mu_ref, w0_ref, w2_ref, a0_ref, a2_ref, g2_ref, kk_ref, ka_ref,
                    r_o, lw_o, k_o, v_o, kk_o, a_o, g_o, carry):
    t = pl.program_id(1)
    tt = x_ref.shape[0]

    @pl.when(t == 0)
    def _():
        carry[...] = sp_ref[0]

    x = x_ref[...]
    xprev = _shift_rows(x, carry[...], 1)
    carry[...] = x[tt - 1:tt, :]
    xs = x + (xprev - x) * mu_ref[...]
    w = RW_WIDTH
    r = xs[:, 0:w]
    k = xs[:, w:2 * w]
    v = xs[:, 2 * w:3 * w]
    wd = xs[:, 3 * w:3 * w + W_LORA]
    ad = xs[:, 3 * w + W_LORA:3 * w + W_LORA + A_LORA]
    gd = xs[:, 3 * w + W_LORA + A_LORA:RW_PAD]
    z = -(w0_ref[...] + _dot(jnp.tanh(wd), w2_ref[...]))
    softplus = jnp.maximum(z, 0.0) + jnp.log(1.0 + jnp.exp(-jnp.abs(z)))
    w_log = -softplus - 0.5
    a = jax.nn.sigmoid(a0_ref[...] + _dot(ad, a2_ref[...]))
    r_o[...] = r
    lw_o[...] = -jnp.exp(w_log)
    k_o[...] = k * (1.0 + (a - 1.0) * ka_ref[...])
    v_o[...] = v
    kk_o[...] = k * kk_ref[...]
    a_o[...] = a
    g_o[...] = _dot(jax.nn.sigmoid(gd), g2_ref[...])


def rw_prep(rw, shift_prev, mu, w0, w2, a0, a2, g2, k_k, k_a, *, nseq, seqlen, tt):
    nt = seqlen // tt
    row = lambda b, t: (b * nt + t, 0)
    const = lambda b, t: (0, 0)
    vec = lambda n: pl.BlockSpec((1, n), const)
    out_sds = jax.ShapeDtypeStruct((nseq * seqlen, RW_WIDTH), F32)
    return pl.pallas_call(
        _rw_prep_kernel,
        grid=(nseq, nt),
        in_specs=[pl.BlockSpec((tt, RW_PAD), row),
                  pl.BlockSpec((1, 1, RW_PAD), lambda b, t: (b, 0, 0)),
                  vec(RW_PAD), vec(RW_WIDTH),
                  pl.BlockSpec((W_LORA, RW_WIDTH), const), vec(RW_WIDTH),
                  pl.BlockSpec((A_LORA, RW_WIDTH), const),
                  pl.BlockSpec((G_PAD, RW_WIDTH), const), vec(RW_WIDTH), vec(RW_WIDTH)],
        out_specs=[pl.BlockSpec((tt, RW_WIDTH), row)] * 7,
        out_shape=[out_sds] * 7,
        scratch_shapes=[pltpu.VMEM((1, RW_PAD), F32)],
        compiler_params=_cparams(("arbitrary", "arbitrary"), 48),
    )(rw, shift_prev, mu, w0, w2, a0, a2, g2, k_k, k_a)


def _pad_rows(x, rows):
    if x.shape[0] == rows:
        return x
    return jnp.concatenate([x, jnp.zeros((rows - x.shape[0], x.shape[1]), x.dtype)], axis=0)


def _scan_kernel(r_ref, lw_ref, k_ref, v_ref, kk_ref, a_ref, g_ref, rk_ref, lnw_ref, lnb_ref, s0_ref,
                 y_ref, so_ref, s_scr, *, chunk, hpb):
    c = chunk
    c2 = 2 * c
    ci = pl.program_id(2)

    @pl.when(ci == 0)
    def _():
        s_scr[...] = s0_ref[0]

    lane = lax.broadcasted_iota(I32, (c, LANES), 1)
    m0 = lane < RW_HEAD
    ei = lax.broadcasted_iota(I32, (LANES, LANES), 0)
    ej = lax.broadcasted_iota(I32, (LANES, LANES), 1)
    same_head = lax.shift_right_logical(ei, 6) == lax.shift_right_logical(ej, 6)
    e_sum = jnp.where(same_head, 1.0, 0.0).astype(F32)
    e_mean = e_sum * (1.0 / RW_HEAD)
    ti = lax.broadcasted_iota(I32, (c, c), 0)
    tj = lax.broadcasted_iota(I32, (c, c), 1)
    tril_incl = jnp.where(ti >= tj, 1.0, 0.0).astype(F32)
    ri = lax.broadcasted_iota(I32, (c2, LANES), 0)
    cj = lax.broadcasted_iota(I32, (c2, LANES), 1)
    rl = jnp.where(ri >= c, ri - c, ri)
    cl = jnp.where(cj >= c, cj - c, cj)
    same_blk = (jnp.where(ri >= c, 1, 0) == jnp.where(cj >= c, 1, 0)) & (cj < c2)
    strict = same_blk & (rl > cl)
    incl = same_blk & (rl >= cl)
    zeros_c = jnp.zeros((c, LANES), F32)
    n_steps = max(1, (c - 1).bit_length())

    pairs = range(hpb)
    sls = [slice(p * LANES, (p + 1) * LANES) for p in pairs]
    each = lambda f, *cols: [f(*xs) for xs in zip(*cols)]
    cat = lambda *xs: jnp.concatenate(xs, axis=0)
    e_sum_b = jnp.concatenate([e_sum, e_sum], axis=0).astype(BF16)
    e_mean_b = jnp.concatenate([e_mean, e_mean], axis=0).astype(BF16)
    tril_b = tril_incl.astype(BF16)
    strict2 = jnp.concatenate([strict, strict], axis=1)
    incl2 = jnp.concatenate([incl, incl], axis=1)

    r = [r_ref[:, sl] for sl in sls]
    lw = [lw_ref[:, sl] for sl in sls]
    k = [k_ref[:, sl] for sl in sls]
    v = [v_ref[:, sl] for sl in sls]
    kkr = [kk_ref[:, sl] for sl in sls]
    a = [a_ref[:, sl] for sl in sls]
    s = [s_scr[p] for p in pairs]

    ss = each(lambda x: _dot_split(x * x, e_sum_b), kkr)
    kk = each(lambda x, q: x / jnp.maximum(jnp.sqrt(q), 1e-12), kkr, ss)
    cum = each(lambda x: _dot_split_rhs(tril_b, x), lw)
    p_incl = each(jnp.exp, cum)
    p_excl = each(lambda x, y_: jnp.exp(x - y_), cum, lw)
    p_inv = each(lambda x: jnp.exp(-x), cum)
    at = each(lambda x, y_: -x * y_, kk, p_excl)
    rt = each(lambda x, y_: x * y_, r, p_incl)
    bt = each(lambda x, y_, z: x * y_ * z, kk, a, p_inv)
    kt = each(lambda x, y_: x * y_, k, p_inv)

    lhs4 = each(lambda x, y_: cat(jnp.where(m0, x, 0.0), jnp.where(m0, 0.0, x),
                                  jnp.where(m0, y_, 0.0), jnp.where(m0, 0.0, y_)), at, rt)
    m1 = each(lambda x, yb, yk: _dot_nt(x, cat(_pad_rows(cat(yb, yb), LANES), _pad_rows(cat(yk, yk), LANES))),
              lhs4, bt, kt)
    m1s = each(lambda x: jnp.where(strict2, x[:c2], 0.0), m1)
    m1i = each(lambda x: jnp.where(incl2, x[c2:], 0.0), m1)
    m2 = each(lambda x, y_, z: _dot_nt(cat(x, y_), z), at, rt, s)
    v2 = each(lambda x: _pad_rows(cat(x, x), LANES), v)
    rhs = each(lambda x, y_, z: cat(x[:c], x[:c]) + _dot(y_[:, LANES:], z), m2, m1s, v2)
    u = each(lambda x: _pad_rows(x, LANES), rhs)
    pw = each(lambda x: _pad_rows(x[:, :LANES], LANES), m1s)
    for step in range(n_steps):
        u = each(lambda x, y_: x + _dot(y_, x), u, pw)
        if step + 1 < n_steps:
            pw = each(lambda x: _dot(x, x), pw)
    u_pair = each(lambda x: jnp.where(m0, x[:c], x[c:c2]), u)
    u2 = each(lambda x: _pad_rows(cat(x, x), LANES), u_pair)
    yst = each(lambda x, uu, vv: _dot(x, cat(uu, vv)), m1i, u2, v2)
    y = each(lambda x, y_: x[c:] + jnp.where(m0, y_[:c], y_[c:]), m2, yst)

    p_end = each(lambda x: x[c - 1:c, :], p_incl)
    xt = each(lambda x, y_: _pad_rows(cat(x, y_), LANES), u_pair, v)
    yk = each(lambda x, y_, z: _pad_rows(cat(x * z, y_ * z), LANES), bt, kt, p_end)
    s_new = each(lambda x, z, xx, yy: x * z + _dot(xx.T, yy), s, p_end, xt, yk)
    for p in pairs:
        s_scr[p] = jnp.where(same_head, s_new[p], 0.0)

    mean = each(lambda x: _dot_split(x, e_mean_b), y)
    d = each(lambda x, y_: x - y_, y, mean)
    var = each(lambda x: _dot_split(x * x, e_mean_b), d)
    bonus = each(lambda x, y_, sl: _dot_split(x * y_ * rk_ref[:, sl], e_sum_b), r, k, sls)
    for p in pairs:
        sl = sls[p]
        yn = d[p] * lax.rsqrt(var[p] + GN_EPS) * lnw_ref[:, sl] + lnb_ref[:, sl]
        y_ref[:, sl] = ((yn + bonus[p] * v[p]) * g_ref[:, sl]).astype(y_ref.dtype)

    @pl.when(ci == pl.num_programs(2) - 1)
    def _():
        so_ref[0] = s_scr[...]


def rw_scan(r, lw, k, v, kk, a, g, r_k, ln_w, ln_b, s0, *, nseq, seqlen, chunk, hpb, y_dtype):
    nc = seqlen // chunk
    npair = RW_HEADS // 2
    width = hpb * LANES
    row = lambda b, h, t: (b * nc + t, h)
    par = lambda b, h, t: (0, h)
    st = lambda b, h, t: (b, h, 0, 0)
    kern = functools.partial(_scan_kernel, chunk=chunk, hpb=hpb)
    return pl.pallas_call(
        kern,
        grid=(nseq, npair // hpb, nc),
        in_specs=[pl.BlockSpec((chunk, width), row)] * 7 + [pl.BlockSpec((1, width), par)] * 3
                 + [pl.BlockSpec((1, hpb, LANES, LANES), st)],
        out_specs=[pl.BlockSpec((chunk, width), row), pl.BlockSpec((1, hpb, LANES, LANES), st)],
        out_shape=[jax.ShapeDtypeStruct((nseq * seqlen, RW_WIDTH), y_dtype),
                   jax.ShapeDtypeStruct((nseq, npair, LANES, LANES), F32)],
        scratch_shapes=[pltpu.VMEM((hpb, LANES, LANES), F32)],
        compiler_params=_cparams(("parallel", "parallel", "arbitrary"), 32),
    )(r, lw, k, v, kk, a, g, r_k, ln_w, ln_b, s0)


def _pair_states(s):
    n = s.shape[0]
    s = s.reshape(n, RW_HEADS // 2, 2, RW_HEAD, RW_HEAD)
    z = jnp.zeros_like(s[:, :, 0])
    top = jnp.concatenate([s[:, :, 0], z], axis=-1)
    bot = jnp.concatenate([z, s[:, :, 1]], axis=-1)
    return jnp.concatenate([top, bot], axis=-2)


def _unpair_states(sp):
    n = sp.shape[0]
    s0 = sp[:, :, :RW_HEAD, :RW_HEAD]
    s1 = sp[:, :, RW_HEAD:, RW_HEAD:]
    return jnp.stack([s0, s1], axis=2).reshape(n, RW_HEADS, RW_HEAD, RW_HEAD)


def _kidx_kernel(x_ref, g_ref, kn_ref, kd_ref):
    x = x_ref[...]
    lane = lax.broadcasted_iota(I32, x.shape, 1)
    xk = jnp.where(lane < IDX_DIM, x, 0.0)
    ms = jnp.sum(xk * xk, axis=-1, keepdims=True) * (1.0 / IDX_DIM)
    xd = xk + pltpu.roll(xk, IDX_DIM, 1)
    kd = (xd * lax.rsqrt(ms + RMS_EPS)) * g_ref[...]
    kd_ref[...] = kd
    kn_ref[...] = kd[:, :IDX_DIM]


def kidx_norm(idx_cols, g, *, tm):
    m = idx_cols.shape[0]
    g2 = jnp.concatenate([g, g]).reshape(1, LANES)
    return pl.pallas_call(
        _kidx_kernel,
        grid=(m // tm,),
        in_specs=[pl.BlockSpec((tm, LANES), lambda i: (i, IDX_HEADS * IDX_DIM // LANES)),
                  pl.BlockSpec((1, LANES), lambda i: (0, 0))],
        out_specs=[pl.BlockSpec((tm, IDX_DIM), lambda i: (i, 0)), pl.BlockSpec((tm, LANES), lambda i: (i, 0))],
        out_shape=[jax.ShapeDtypeStruct((m, IDX_DIM), F32), jax.ShapeDtypeStruct((m, LANES), F32)],
        compiler_params=_cparams(("parallel",), 16),
    )(idx_cols, g2)


def _sort_key(score):
    score = jnp.where(score == 0.0, 0.0, score)
    bits = pltpu.bitcast(score, I32)
    return bits ^ (lax.shift_right_arithmetic(bits, 31) & 0x7FFFFFFF)


def _count_lanes(acc):
    return jnp.dot(acc.astype(BF16), jnp.ones((LANES, LANES), BF16), preferred_element_type=F32)


def _topk_select(key_tile, cut_ref, *, rows, n_tiles, topk, col_bits):
    def count(pred):
        acc = jnp.zeros((rows, LANES), F32)
        for t in range(n_tiles):
            acc = acc + jnp.where(pred(key_tile(t), t), 1.0, 0.0)
        return _count_lanes(acc)

    def bit_step(it, thr):
        cand = thr + lax.shift_left(jnp.int32(1), 31 - it)
        cnt = count(lambda kt, t: kt >= cand)
        return jnp.where(cnt >= topk, cand, thr)

    thr = lax.fori_loop(0, 32, bit_step, jnp.full((rows, LANES), INT_MIN, I32))
    n_gt = count(lambda kt, t: kt > thr)
    n_ge = count(lambda kt, t: kt >= thr)
    excess = (n_ge > topk) & (thr > NEG_INF_KEY)
    cut_ref[...] = jnp.full((rows, LANES), BIG_COL, I32)
    lane = lax.broadcasted_iota(I32, (rows, LANES), 1)

    @pl.when(jnp.max(jnp.where(excess, 1.0, 0.0)) > 0.0)
    def _():
        need = topk - n_gt

        def col_step(it, cut):
            cand = cut + lax.shift_left(jnp.int32(1), col_bits - 1 - it)
            cnt = count(lambda kt, t: (kt == thr) & (lane + t * LANES < cand))
            return jnp.where(cnt < need, cand, cut)

        cut = lax.fori_loop(0, col_bits, col_step, jnp.zeros((rows, LANES), I32))
        cut_ref[...] = jnp.where(excess, cut, BIG_COL)

    return thr


def _selected(kt, t, thr, cut):
    lane = lax.broadcasted_iota(I32, kt.shape, 1)
    return ((kt > thr) | ((kt == thr) & (lane + t * LANES <= cut))) & (kt != NEG_INF_KEY)


def _idx_scores(q_ref, wq_ref, kd, *, rows):
    lane = lax.broadcasted_iota(I32, (rows, LANES), 1)
    lo = lane < IDX_DIM
    k_hi, k_lo = _split_bf16(kd)
    k_cat = jnp.concatenate([k_hi, k_lo], axis=1)
    zero = jnp.zeros((rows, LANES), BF16)
    score = jnp.zeros((rows, kd.shape[0]), F32)
    for hp in range(IDX_HEADS // 2):
        qp = q_ref[:, hp * LANES:(hp + 1) * LANES]
        q_hi = qp.astype(BF16)
        q_lo = pltpu.roll(qp - q_hi.astype(F32), IDX_DIM, 1).astype(BF16)
        for half in range(2):
            h = 2 * hp + half
            own = lo if half == 0 else jnp.logical_not(lo)
            lhs = jnp.concatenate([jnp.where(own, q_hi, q_lo), jnp.where(own, q_hi, zero)], axis=1)
            d = lax.dot_general(lhs, k_cat, (((1,), (1,)), ((), ())), preferred_element_type=F32)
            wh = wq_ref[:, IDX_DIM + h:IDX_DIM + h + 1] * IDX_SCALE
            score = score + wh * jnp.maximum(d, 0.0)
    return score


def _dot_nt3(a, b_hi, b_lo):
    a_hi, a_lo = _split_bf16(a)
    nt = lambda x, y: lax.dot_general(x, y, (((1,), (1,)), ((), ())), preferred_element_type=F32)
    return nt(a_hi, b_hi) + (nt(a_hi, b_lo) + nt(a_lo, b_hi))


def _prompt_index_kernel(q_ref, wq_ref, kd_ref, m_ref, key_scr, cut_scr, *, tq, q0, kv, n_tiles, topk):
    cols = n_tiles * LANES
    q_pos = q0 + lax.broadcasted_iota(I32, (tq, cols), 0)
    score = _idx_scores(q_ref, wq_ref, kd_ref[0], rows=tq)
    col = lax.broadcasted_iota(I32, (tq, cols), 1)
    key_scr[...] = _sort_key(jnp.where(col <= q_pos, score, -jnp.inf))
    key_tile = lambda t: key_scr[:, t * LANES:(t + 1) * LANES]
    thr = _topk_select(key_tile, cut_scr, rows=tq, n_tiles=n_tiles, topk=topk, col_bits=12)
    cut = cut_scr[...]
    for t in range(n_tiles):
        width = min(LANES, kv - t * LANES)
        sel = jnp.where(_selected(key_tile(t), t, thr, cut), 1.0, 0.0).astype(m_ref.dtype)
        m_ref[:, t * LANES:t * LANES + width] = sel[:, :width]


def prompt_index_mask(idx_cols, kdup_pad, *, nseq, seqlen, tq, qi, topk):
    nq = seqlen // tq
    kv = (qi + 1) * tq
    n_tiles = -(-kv // LANES)
    cols = n_tiles * LANES
    assert cols <= kdup_pad.shape[1]
    kern = functools.partial(_prompt_index_kernel, tq=tq, q0=qi * tq, kv=kv, n_tiles=n_tiles, topk=topk)
    return pl.pallas_call(
        kern,
        grid=(nseq,),
        in_specs=[pl.BlockSpec((tq, IDX_HEADS * IDX_DIM), lambda b: (b * nq + qi, 0)),
                  pl.BlockSpec((tq, LANES), lambda b: (b * nq + qi, IDX_HEADS * IDX_DIM // LANES)),
                  pl.BlockSpec((1, cols, LANES), lambda b: (b, 0, 0))],
        out_specs=pl.BlockSpec((tq, kv), lambda b: (b, 0)),
        out_shape=jax.ShapeDtypeStruct((nseq * tq, kv), BF16),
        scratch_shapes=[pltpu.VMEM((tq, cols), I32), pltpu.VMEM((tq, LANES), I32)],
        compiler_params=_cparams(("parallel",)),
    )(idx_cols, idx_cols, kdup_pad)


def _prompt_attn_kernel(q_ref, k_ref, v_ref, m_ref, o_ref):
    logits = _dot_nt(q_ref[...] * ATT_SCALE, k_ref[0])
    logits = jnp.where(m_ref[...] > 0, logits, -jnp.inf)
    mx = jnp.max(logits, axis=-1, keepdims=True)
    e = jnp.exp(logits - mx)
    inv = 1.0 / jnp.sum(e, axis=-1, keepdims=True)
    o_ref[0] = (_dot(e, v_ref[0]) * inv).astype(o_ref.dtype)


def prompt_attention(qkv, mask, *, nseq, seqlen, tq, qi):
    nq = seqlen // tq
    kv = (qi + 1) * tq
    qkv3 = qkv.reshape(nseq, seqlen, 3 * ATT_WIDTH)
    return pl.pallas_call(
        _prompt_attn_kernel,
        grid=(nseq, ATT_HEADS),
        in_specs=[pl.BlockSpec((tq, ATT_HEAD), lambda b, h: (b * nq + qi, h)),
                  pl.BlockSpec((1, kv, ATT_HEAD), lambda b, h: (b, 0, ATT_HEADS + h)),
                  pl.BlockSpec((1, kv, ATT_HEAD), lambda b, h: (b, 0, 2 * ATT_HEADS + h)),
                  pl.BlockSpec((tq, kv), lambda b, h: (b, 0))],
        out_specs=pl.BlockSpec((1, tq, ATT_HEAD), lambda b, h: (b, 0, h)),
        out_shape=jax.ShapeDtypeStruct((nseq, tq, ATT_WIDTH), BF16),
        compiler_params=_cparams(("parallel", "arbitrary")),
    )(qkv, qkv3, qkv3, mask)


PAGES_PER_STEP = 8


def _sample_score_kernel(pt_ref, q_ref, w_ref, *rest, dsq):
    page_refs, o_ref = rest[:-1], rest[-1]
    q = q_ref[0]
    w = w_ref[0]
    q_hi, q_lo = _split_bf16(q)
    mm = lambda a, b_: jnp.dot(a, b_, preferred_element_type=F32)
    for j, page_ref in enumerate(page_refs):
        p_hi, p_lo = _split_bf16(page_ref[0, 0])
        d = jnp.maximum(mm(q_hi, p_hi) + (mm(q_hi, p_lo) + mm(q_lo, p_hi)), 0.0) * w
        o_ref[0, :, j * PAGE_SIZE:(j + 1) * PAGE_SIZE] = jnp.sum(d.reshape(IDX_HEADS, dsq, PAGE_SIZE), axis=0)


def sample_page_scores(page_table, qh, wh, cache_kidx, layer, *, nb, dsq, n_pages):
    kern = functools.partial(_sample_score_kernel, dsq=dsq)
    rows = IDX_HEADS * dsq
    pps = PAGES_PER_STEP
    assert n_pages % pps == 0

    cache_t = jnp.swapaxes(cache_kidx, 2, 3)

    def page_spec(j):
        return pl.BlockSpec((1, 1, IDX_DIM, PAGE_SIZE),
                            lambda b, p, pt: (layer, pt[b * n_pages + p * pps + j], 0, 0))

    grid_spec = pltpu.PrefetchScalarGridSpec(
        num_scalar_prefetch=1,
        grid=(nb, n_pages // pps),
        in_specs=[pl.BlockSpec((1, rows, IDX_DIM), lambda b, p, pt: (b, 0, 0)),
                  pl.BlockSpec((1, rows, 1), lambda b, p, pt: (b, 0, 0))] + [page_spec(j) for j in range(pps)],
        out_specs=pl.BlockSpec((1, dsq, pps * PAGE_SIZE), lambda b, p, pt: (b, 0, p)),
    )
    return pl.pallas_call(
        kern,
        grid_spec=grid_spec,
        out_shape=jax.ShapeDtypeStruct((nb, dsq, n_pages * PAGE_SIZE), F32),
        compiler_params=_cparams(("parallel", "arbitrary"), 16),
    )(page_table.reshape(-1), qh, wh, *([cache_t] * pps))


def _sample_select_kernel(sc_ref, q_ref, w_ref, kn_ref, idx_ref, key_scr, keyq_scr, rank_scr, cut_scr,
                          *, dsq, past, topk, nt_pad):
    n_past_tiles = past // LANES
    n_tiles = n_past_tiles + 1
    keyq_scr[...] = jnp.full(keyq_scr.shape, NEG_INF_KEY, I32)

    def put_tile(t, keys):
        key_scr[t] = keys
        for q in range(dsq):
            keyq_scr[q, t:t + 1, :] = keys[q:q + 1, :]

    for t in range(n_past_tiles):
        put_tile(t, _sort_key(sc_ref[0, :, t * LANES:(t + 1) * LANES]))
    kn_hi, kn_lo = _split_bf16(kn_ref[0])
    d = jnp.maximum(_dot_nt3(q_ref[0], kn_hi, kn_lo), 0.0) * w_ref[0]
    s_new = jnp.sum(d.reshape(IDX_HEADS, dsq, LANES), axis=0)
    qrow = lax.broadcasted_iota(I32, (dsq, LANES), 0)
    jcol = lax.broadcasted_iota(I32, (dsq, LANES), 1)
    put_tile(n_past_tiles, _sort_key(jnp.where(jcol <= qrow, s_new, -jnp.inf)))
    thr = _topk_select(lambda t: key_scr[t], cut_scr, rows=dsq, n_tiles=n_tiles, topk=topk, col_bits=15)
    cut = cut_scr[...]

    li = lax.broadcasted_iota(I32, (LANES, LANES), 0)
    lj = lax.broadcasted_iota(I32, (LANES, LANES), 1)
    upper = jnp.where(li < lj, 1.0, 0.0).astype(BF16)
    ones = jnp.ones((LANES, LANES), BF16)
    ti = lax.broadcasted_iota(I32, (nt_pad, nt_pad), 0)
    tj = lax.broadcasted_iota(I32, (nt_pad, nt_pad), 1)
    earlier = jnp.where(tj < ti, 1.0, 0.0).astype(BF16)
    col = (lax.broadcasted_iota(I32, (nt_pad, LANES), 0) * LANES + lax.broadcasted_iota(I32, (nt_pad, LANES), 1))
    slot_f = lax.broadcasted_iota(I32, (topk, LANES), 0).astype(F32)
    lane_f = lax.broadcasted_iota(I32, (topk, LANES), 1).astype(F32)
    for q in range(dsq):
        kq = keyq_scr[q]
        thr_q = thr[q:q + 1]
        cut_q = cut[q:q + 1]
        sel = ((kq > thr_q) | ((kq == thr_q) & (col <= cut_q))) & (kq != NEG_INF_KEY)
        selb = jnp.where(sel, 1.0, 0.0).astype(BF16)
        within = jnp.dot(selb, upper, preferred_element_type=F32)
        totals = jnp.dot(selb, ones, preferred_element_type=F32)
        before = jnp.dot(earlier, totals.astype(BF16), preferred_element_type=F32)
        rank_scr[...] = jnp.where(sel, before + within, -1.0)

        def tile_step(t, acc):
            rk = jnp.broadcast_to(rank_scr[pl.ds(t, 1), :], (topk, LANES))
            return acc + jnp.where(rk == slot_f, lane_f + jnp.asarray(t * LANES).astype(F32), 0.0)

        acc = lax.fori_loop(0, n_tiles, tile_step, jnp.zeros((topk, LANES), F32))
        idx_ref[0, q] = jnp.sum(acc, axis=1, keepdims=True).astype(I32)


def sample_select(scores, qh, wh, kn_pad, *, nb, dsq, past, topk):
    rows = IDX_HEADS * dsq
    n_tiles = past // LANES + 1
    nt_pad = -(-n_tiles // LANES) * LANES
    kern = functools.partial(_sample_select_kernel, dsq=dsq, past=past, topk=topk, nt_pad=nt_pad)
    return pl.pallas_call(
        kern,
        grid=(nb,),
        in_specs=[pl.BlockSpec((1, dsq, past), lambda b: (b, 0, 0)),
                  pl.BlockSpec((1, rows, IDX_DIM), lambda b: (b, 0, 0)),
                  pl.BlockSpec((1, rows, 1), lambda b: (b, 0, 0)),
                  pl.BlockSpec((1, LANES, IDX_DIM), lambda b: (b, 0, 0))],
        out_specs=pl.BlockSpec((1, dsq, topk, 1), lambda b: (b, 0, 0, 0)),
        out_shape=jax.ShapeDtypeStruct((nb, dsq, topk, 1), I32),
        scratch_shapes=[pltpu.VMEM((n_tiles, dsq, LANES), I32), pltpu.VMEM((dsq, nt_pad, LANES), I32),
                        pltpu.VMEM((nt_pad, LANES), F32), pltpu.VMEM((dsq, LANES), I32)],
        compiler_params=_cparams(("parallel",), 32),
    )(scores, qh, wh, kn_pad)


GATHER_UNROLL = 8


def _sample_attn_kernel(idx_ref, pt_ref, q_ref, pos_ref, kn_ref, vn_ref, ck_ref, cv_ref, o_ref,
                        kbuf, vbuf, sem, *, layer, dsq, past, n_pages, topk):
    b = pl.program_id(0)

    def start_query(q, slot):
        def body(j8, pos_max):
            for jj in range(GATHER_UNROLL):
                j = j8 * GATHER_UNROLL + jj
                pos = idx_ref[(b * dsq + q) * topk + j]
                ppos = jnp.minimum(pos, past - 1)
                page = pt_ref[b * n_pages + lax.shift_right_logical(ppos, 7)]
                off = ppos & (PAGE_SIZE - 1)
                pltpu.make_async_copy(ck_ref.at[layer, page, off], kbuf.at[slot, :, j], sem.at[0, slot]).start()
                pltpu.make_async_copy(cv_ref.at[layer, page, off], vbuf.at[slot, :, j], sem.at[1, slot]).start()
                pos_max = jnp.maximum(pos_max, pos)
            return pos_max

        return lax.fori_loop(0, topk // GATHER_UNROLL, body, jnp.int32(0))

    def wait_query(slot):
        pltpu.make_async_copy(kbuf.at[1 - slot], kbuf.at[slot], sem.at[0, slot]).wait()
        pltpu.make_async_copy(vbuf.at[1 - slot], vbuf.at[slot], sem.at[1, slot]).wait()

    def patch_new_rows(q, slot):
        rel = pos_ref[0, q] - past
        lane = lax.broadcasted_iota(I32, (topk, LANES), 1)
        onehot = jnp.where(rel == lane, 1.0, 0.0).astype(BF16)
        is_new = rel >= 0
        pad = jnp.zeros((LANES - dsq, ATT_HEAD), F32)
        for h in range(ATT_HEADS):
            for buf, new_ref in ((kbuf, kn_ref), (vbuf, vn_ref)):
                rows = jnp.concatenate([new_ref[0, :, h, :], pad], axis=0)
                hi, lo = _split_bf16(rows)
                fix = (jnp.dot(onehot, hi, preferred_element_type=F32) + jnp.dot(onehot, lo, preferred_element_type=F32))
                buf[slot, h] = jnp.where(is_new, fix, buf[slot, h])

    pos_max = start_query(0, 0)
    for q in range(dsq):
        slot = q % 2
        next_max = start_query(q + 1, 1 - slot) if q + 1 < dsq else None
        wait_query(slot)

        @pl.when(pos_max >= past)
        def _():
            patch_new_rows(q, slot)

        heads = range(ATT_HEADS)
        qs = [jnp.broadcast_to(q_ref[0, q, h:h + 1, :], (8, ATT_HEAD)) for h in heads]
        logits = jnp.concatenate([_dot_nt(qs[h], kbuf[slot, h])[0:1] for h in heads], axis=0) * ATT_SCALE
        mx = jnp.max(logits, axis=-1, keepdims=True)
        e = jnp.exp(logits - mx)
        prob = e / jnp.sum(e, axis=-1, keepdims=True)
        outs = [_dot(jnp.broadcast_to(prob[h:h + 1], (8, topk)), vbuf[slot, h])[0:1] for h in heads]
        o_ref[0, q:q + 1, :] = jnp.concatenate(outs, axis=1)
        pos_max = next_max


def sample_attention(sel_idx, page_table, q, k_new, v_new, cache_k, cache_v, layer, *, nb, dsq, past, topk):
    n_pages = past // PAGE_SIZE
    assert topk % GATHER_UNROLL == 0
    kern = functools.partial(_sample_attn_kernel, layer=layer, dsq=dsq, past=past, n_pages=n_pages, topk=topk)
    new_spec = pl.BlockSpec((1, dsq, ATT_HEADS, ATT_HEAD), lambda b, idx, pt: (b, 0, 0, 0))
    grid_spec = pltpu.PrefetchScalarGridSpec(
        num_scalar_prefetch=2,
        grid=(nb,),
        in_specs=[new_spec,
                  pl.BlockSpec((1, dsq, topk, 1), lambda b, idx, pt: (b, 0, 0, 0)),
                  new_spec, new_spec,
                  pl.BlockSpec(memory_space=pl.ANY), pl.BlockSpec(memory_space=pl.ANY)],
        out_specs=pl.BlockSpec((1, dsq, ATT_WIDTH), lambda b, idx, pt: (b, 0, 0)),
        scratch_shapes=[pltpu.VMEM((2, ATT_HEADS, topk, ATT_HEAD), F32),
                        pltpu.VMEM((2, ATT_HEADS, topk, ATT_HEAD), F32),
                        pltpu.SemaphoreType.DMA((2, 2))],
    )
    return pl.pallas_call(
        kern,
        grid_spec=grid_spec,
        out_shape=jax.ShapeDtypeStruct((nb, dsq, ATT_WIDTH), F32),
        compiler_params=_cparams(("arbitrary",), 32),
    )(sel_idx.reshape(-1), page_table.reshape(-1), q, sel_idx, k_new, v_new, cache_k, cache_v)


def _conv_gate_kernel(ug_ref, uv_ref, pg_ref, pv_ref, wg_ref, wv_ref, bg_ref, bv_ref, o_ref, *, nb, seqlen):
    def conv(u, prev, w, bias):
        return w[0:1, :] * _shift_rows(u, prev, 2) + w[1:2, :] * _shift_rows(u, prev[1:2, :], 1) + w[2:3, :] * u + bias

    outs = []
    for s in range(nb):
        rows = slice(s * seqlen, (s + 1) * seqlen)
        gate = conv(ug_ref[rows, :], pg_ref[s], wg_ref[...], bg_ref[...])
        val = conv(uv_ref[rows, :], pv_ref[s], wv_ref[...], bv_ref[...])
        outs.append(gate * jax.nn.sigmoid(gate) * val)
    o_ref[...] = (outs[0] if nb == 1 else jnp.concatenate(outs, axis=0)).astype(o_ref.dtype)


def conv_gate(u, conv_prev, conv_w, conv_b, *, nseq, seqlen, nb, tc):
    m = nseq * seqlen
    half = D_FF // tc
    kern = functools.partial(_conv_gate_kernel, nb=nb, seqlen=seqlen)
    rows = nb * seqlen
    return pl.pallas_call(
        kern,
        grid=(nseq // nb, half),
        in_specs=[pl.BlockSpec((rows, tc), lambda b, j: (b, j)),
                  pl.BlockSpec((rows, tc), lambda b, j: (b, j + half)),
                  pl.BlockSpec((nb, CONV_W - 1, tc), lambda b, j: (b, 0, j)),
                  pl.BlockSpec((nb, CONV_W - 1, tc), lambda b, j: (b, 0, j + half)),
                  pl.BlockSpec((CONV_W, tc), lambda b, j: (0, j)),
                  pl.BlockSpec((CONV_W, tc), lambda b, j: (0, j + half)),
                  pl.BlockSpec((1, tc), lambda b, j: (0, j)),
                  pl.BlockSpec((1, tc), lambda b, j: (0, j + half))],
        out_specs=pl.BlockSpec((rows, tc), lambda b, j: (b, j)),
        out_shape=jax.ShapeDtypeStruct((m, D_FF), BF16),
        compiler_params=_cparams(("parallel", "arbitrary"), 48),
    )(u, u, conv_prev, conv_prev, conv_w, conv_w, conv_b, conv_b)


QKV_COL0 = RW_PAD
IDX_COL0 = QKV_COL0 + 3 * ATT_WIDTH
GATE_COL0 = IDX_COL0 + IDX_PAD
W_IN_PAD = GATE_COL0 + 2 * D_MODEL


def _realign_kernel(w_ref, o_ref):
    x = w_ref[0]
    rows = x.shape[0]
    g0 = RW_COLS + ATT_COLS
    o_ref[0, :, 0:RW_COLS] = x[:, 0:RW_COLS].astype(BF16)
    o_ref[0, :, RW_COLS:RW_PAD] = jnp.zeros((rows, RW_PAD - RW_COLS), BF16)
    o_ref[0, :, QKV_COL0:QKV_COL0 + ATT_COLS] = x[:, RW_COLS:g0].astype(BF16)
    o_ref[0, :, QKV_COL0 + ATT_COLS:GATE_COL0] = jnp.zeros((rows, GATE_COL0 - QKV_COL0 - ATT_COLS), BF16)
    o_ref[0, :, GATE_COL0:W_IN_PAD] = x[:, g0:].astype(BF16)


def _realign_w_in(w_in, *, tk=128):
    layers, d, n = w_in.shape
    return pl.pallas_call(
        _realign_kernel,
        grid=(layers, d // tk),
        in_specs=[pl.BlockSpec((1, tk, n), lambda l, i: (l, i, 0))],
        out_specs=pl.BlockSpec((1, tk, W_IN_PAD), lambda l, i: (l, i, 0)),
        out_shape=jax.ShapeDtypeStruct((layers, d, W_IN_PAD), BF16),
        compiler_params=_cparams(("parallel", "parallel")),
    )(w_in)


def _layer_weights(l, rw_mu, rw_g2):
    mu = jnp.pad(rw_mu[l], (0, RW_PAD - RW_COLS)).reshape(1, RW_PAD)
    g2 = jnp.pad(rw_g2[l], ((0, G_PAD - G_LORA), (0, 0)))
    return mu, g2


def _trunk_layer(x, l, p, lw, shift_prev, wkv_prev, conv_prev, attend, *, nseq, seqlen, tiles):
    m = nseq * seqlen
    mu, g2 = lw
    w_all = p['w_in_bf16']
    row = lambda a: a.reshape(1, -1)
    h = rmsnorm_rows(x, p['norm_mix'][l], tm=tiles['rms'], out_dtype=BF16)
    proj = lambda col0, n, tn: matmul(h, w_all, layer=l, col0=col0, n_cols=n, tm=tiles['tm'], tn=tn)
    rw = proj(0, RW_PAD, tiles['tn_rw'])
    qkv = proj(QKV_COL0, 3 * ATT_WIDTH, tiles['tn_qkv'])
    idx_cols = proj(IDX_COL0, IDX_PAD, tiles['tn_idx'])
    gates = proj(GATE_COL0, 2 * D_MODEL, tiles['tn_gate'])

    sp = jnp.pad(shift_prev, ((0, 0), (0, 0), (0, RW_PAD - RW_COLS)))
    r, lwd, k2, v, kk, a, g = rw_prep(rw, sp, mu, row(p['rw_w0'][l]), p['rw_w2'][l], row(p['rw_a0'][l]),
                                       p['rw_a2'][l], g2, row(p['rw_k_k'][l]), row(p['rw_k_a'][l]),
                                       nseq=nseq, seqlen=seqlen, tt=tiles['chunk'])
    y_a, s_pair = rw_scan(r, lwd, k2, v, kk, a, g, row(p['rw_r_k'][l]), row(p['rw_ln_w'][l]), row(p['rw_ln_b'][l]),
                          _pair_states(wkv_prev), nseq=nseq, seqlen=seqlen, chunk=tiles['chunk'], hpb=tiles['hpb'],
                          y_dtype=tiles['y_dtype'])
    y_a = y_a.astype(BF16)
    wkv_new = _unpair_states(s_pair)
    shift_new = rw.reshape(nseq, seqlen, RW_PAD)[:, seqlen - 1:, :RW_COLS]

    ki_n, ki_dup = kidx_norm(idx_cols, p['idx_k_norm'][l], tm=tiles['kidx'])
    y_b = attend(qkv, idx_cols, ki_n, ki_dup)

    mix = branch_mix(y_a, y_b, p['w_branch_a'], p['w_branch_b'], gates, tm=tiles['tm'], tn=tiles['tn_mix'], layer=l)
    x = matmul(mix, p['w_out'], layer=l, tm=tiles['tm'], tn=tiles['tn_out'], residual=x)

    h2 = rmsnorm_rows(x, p['norm_ffn'][l], tm=tiles['rms'], out_dtype=BF16)
    u = matmul(h2, p['w_up'], layer=l, tm=tiles['tm'], tn=tiles['tn_up'])
    act = conv_gate(u, conv_prev, p['conv_w'][l], row(p['conv_b'][l]), nseq=nseq, seqlen=seqlen,
                    nb=tiles['conv_nb'], tc=tiles['conv_tc'])
    u3 = u.reshape(nseq, seqlen, 2 * D_FF)
    conv_new = jnp.concatenate([conv_prev, u3], axis=1)[:, seqlen:] if seqlen < CONV_W - 1 else u3[:, seqlen - (CONV_W - 1):]
    x = matmul(act, p['w_down'], layer=l, tm=tiles['tm_down'], tn=tiles['tn_down'], residual=x)

    k_new = qkv[:, ATT_WIDTH:2 * ATT_WIDTH].reshape(nseq, seqlen, ATT_HEADS, ATT_HEAD)
    v_new = qkv[:, 2 * ATT_WIDTH:].reshape(nseq, seqlen, ATT_HEADS, ATT_HEAD)
    return x, (k_new, v_new, ki_n.reshape(nseq, seqlen, IDX_DIM), wkv_new, shift_new, conv_new)


PROMPT_TILES = dict(rms=688, tm=1376, tn_rw=768, tn_qkv=768, tn_idx=256, tn_gate=512, tn_mix=256, tn_out=256, tn_up=512,
                    tm_down=688, tn_down=256, chunk=48, hpb=16, kidx=688, conv_nb=1, conv_tc=256, tq=688, y_dtype=BF16)
SAMPLE_TILES = dict(rms=64, tm=64, tn_rw=1152, tn_qkv=768, tn_idx=256, tn_gate=1024, tn_mix=1024, tn_out=1024, tn_up=512,
                    tm_down=64, tn_down=512, chunk=8, hpb=4, kidx=64, conv_nb=8, conv_tc=256, y_dtype=F32)


def kernel(x_prompt, x_sample, cache_k, cache_v, cache_kidx, state_wkv, state_shift, state_conv, page_table,
           meta_tokens, norm_mix, w_in, rw_mu, rw_w0, rw_w2, rw_a0, rw_a2, rw_g2, rw_k_k, rw_k_a, rw_r_k,
           rw_ln_w, rw_ln_b, idx_k_norm, w_branch_a, w_branch_b, w_out, norm_ffn, w_up, conv_w, conv_b,
           w_down, norm_final):
    p = {'norm_mix': norm_mix, 'rw_w0': rw_w0, 'rw_w2': rw_w2, 'rw_a0': rw_a0, 'rw_a2': rw_a2,
         'rw_k_k': rw_k_k, 'rw_k_a': rw_k_a, 'rw_r_k': rw_r_k.reshape(rw_r_k.shape[0], RW_WIDTH),
         'rw_ln_w': rw_ln_w, 'rw_ln_b': rw_ln_b, 'idx_k_norm': idx_k_norm, 'w_branch_a': w_branch_a,
         'w_branch_b': w_branch_b, 'w_out': w_out, 'norm_ffn': norm_ffn, 'w_up': w_up, 'conv_w': conv_w,
         'conv_b': conv_b, 'w_down': w_down, 'w_in_bf16': _realign_w_in(w_in)}
    depth = w_in.shape[0]
    nb_p, seq = x_prompt.shape[:2]
    tp = seq + N_META
    nb_s, ds = x_sample.shape[:2]
    n_pages = page_table.shape[1]
    past = n_pages * PAGE_SIZE

    meta = jnp.broadcast_to(meta_tokens[None], (nb_p, N_META, D_MODEL))
    xp = jnp.concatenate([meta, x_prompt], axis=1).reshape(nb_p * tp, D_MODEL)
    xs = x_sample.reshape(nb_s * ds, D_MODEL)

    topk_p = min(TOPK_MAX, tp // 4)
    topk_s = min(TOPK_MAX, (past + ds) // 4)
    lp = -(-tp // LANES) * LANES
    tq = PROMPT_TILES['tq']

    def prompt_attend(qkv, idx_cols, ki_n, ki_dup):
        kd = jnp.pad(ki_dup.reshape(nb_p, tp, LANES), ((0, 0), (0, lp - tp), (0, 0)))
        outs = []
        for qi in range(tp // tq):
            mask = prompt_index_mask(idx_cols, kd, nseq=nb_p, seqlen=tp, tq=tq, qi=qi, topk=topk_p)
            outs.append(prompt_attention(qkv, mask, nseq=nb_p, seqlen=tp, tq=tq, qi=qi))
        return jnp.concatenate(outs, axis=1).reshape(nb_p * tp, ATT_WIDTH)

    def make_sample_attend(l):
        def attend(qkv, idx_cols, ki_n, ki_dup):
            qi = idx_cols[:, :IDX_HEADS * IDX_DIM].reshape(nb_s, ds, IDX_HEADS, IDX_DIM)
            qh = jnp.swapaxes(qi, 1, 2).reshape(nb_s, IDX_HEADS * ds, IDX_DIM)
            wi = idx_cols[:, IDX_HEADS * IDX_DIM + IDX_DIM:IDX_HEADS * IDX_DIM + IDX_DIM + IDX_HEADS]
            wh = jnp.swapaxes(wi.reshape(nb_s, ds, IDX_HEADS), 1, 2).reshape(nb_s, IDX_HEADS * ds, 1) * IDX_SCALE
            scores = sample_page_scores(page_table, qh, wh, cache_kidx, l, nb=nb_s, dsq=ds, n_pages=n_pages)
            kn_pad = jnp.pad(ki_n.reshape(nb_s, ds, IDX_DIM), ((0, 0), (0, LANES - ds), (0, 0)))
            sel = sample_select(scores, qh, wh, kn_pad, nb=nb_s, dsq=ds, past=past, topk=topk_s)
            q = qkv[:, :ATT_WIDTH].reshape(nb_s, ds, ATT_HEADS, ATT_HEAD)
            k_new = qkv[:, ATT_WIDTH:2 * ATT_WIDTH].reshape(nb_s, ds, ATT_HEADS, ATT_HEAD)
            v_new = qkv[:, 2 * ATT_WIDTH:].reshape(nb_s, ds, ATT_HEADS, ATT_HEAD)
            y = sample_attention(sel, page_table, q, k_new, v_new, cache_k, cache_v, l,
                                 nb=nb_s, dsq=ds, past=past, topk=topk_s)
            return y.reshape(nb_s * ds, ATT_WIDTH).astype(BF16)
        return attend

    zero_shift = jnp.zeros((nb_p, 1, RW_COLS), F32)
    zero_wkv = jnp.zeros((nb_p, RW_HEADS, RW_HEAD, RW_HEAD), F32)
    zero_conv = jnp.zeros((nb_p, CONV_W - 1, 2 * D_FF), F32)
    sts_p, sts_s = [], []
    for l in range(depth):
        lw = _layer_weights(l, rw_mu, rw_g2)
        xp, st = _trunk_layer(xp, l, p, lw, zero_shift, zero_wkv, zero_conv, prompt_attend,
                              nseq=nb_p, seqlen=tp, tiles=PROMPT_TILES)
        sts_p.append(st)
        xs, st = _trunk_layer(xs, l, p, lw, state_shift[l], state_wkv[l], state_conv[l], make_sample_attend(l),
                              nseq=nb_s, seqlen=ds, tiles=SAMPLE_TILES)
        sts_s.append(st)

    y_prompt = rmsnorm_rows(xp, norm_final, tm=PROMPT_TILES['rms'], out_dtype=F32)
    y_prompt = y_prompt.reshape(nb_p, tp, D_MODEL)[:, N_META:]
    y_sample = rmsnorm_rows(xs, norm_final, tm=SAMPLE_TILES['rms'], out_dtype=F32).reshape(nb_s, ds, D_MODEL)
    stk = lambda sts, i: jnp.stack([s[i] for s in sts])
    return (y_prompt, y_sample,
            stk(sts_p, 0), stk(sts_p, 1), stk(sts_p, 2), stk(sts_p, 3), stk(sts_p, 4), stk(sts_p, 5),
            stk(sts_s, 0), stk(sts_s, 1), stk(sts_s, 2), stk(sts_s, 3), stk(sts_s, 4), stk(sts_s, 5))
```

```python
import functools

import jax
import jax.numpy as jnp
from jax import lax
from jax.experimental import pallas as pl
from jax.experimental.pallas import tpu as pltpu

F32 = jnp.float32
BF16 = jnp.bfloat16
I32 = jnp.int32
HIGHEST = lax.Precision.HIGHEST

D_MODEL = 4096
N_META = 16
PAGE_SIZE = 128
RW_HEAD = 64
RW_HEADS = 32
RW_WIDTH = RW_HEADS * RW_HEAD
W_LORA = 128
A_LORA = 128
G_LORA = 480
RW_COLS = 3 * RW_WIDTH + W_LORA + A_LORA + G_LORA
ATT_HEADS = 16
ATT_HEAD = 128
ATT_WIDTH = ATT_HEADS * ATT_HEAD
IDX_HEADS = 16
IDX_DIM = 64
TOPK_MAX = 256
ATT_COLS = 3 * ATT_WIDTH + IDX_HEADS * IDX_DIM + IDX_DIM + IDX_HEADS
D_FF = 11008
CONV_W = 3
RMS_EPS = 1e-6
GN_EPS = 64e-5
ATT_SCALE = ATT_HEAD ** -0.5
IDX_SCALE = (IDX_HEADS ** -0.5) * (IDX_DIM ** -0.5)

LANES = 128
RW_PAD = 6912
IDX_PAD = 1280
G_PAD = 512
INT_MIN = -2 ** 31
NEG_INF_KEY = INT_MIN + 0x7FFFFF
BIG_COL = 2 ** 30
VMEM_MB = 60


def _cparams(sem, mb=VMEM_MB):
    return pltpu.CompilerParams(dimension_semantics=sem, vmem_limit_bytes=mb * 1024 * 1024)


def _dot(a, b):
    return jnp.dot(a.astype(BF16), b.astype(BF16), preferred_element_type=F32)


def _dot_nt(a, b, precision=None):
    if precision is None:
        a, b = a.astype(BF16), b.astype(BF16)
    return lax.dot_general(a, b, (((1,), (1,)), ((), ())), precision=precision,
                           preferred_element_type=F32)


def _dot_hi(a, b):
    return jnp.dot(a, b, precision=HIGHEST, preferred_element_type=F32)


def _split_bf16(x):
    hi = x.astype(BF16)
    return hi, (x - hi.astype(F32)).astype(BF16)


def _dot_split(x, m2_bf16):
    hi, lo = _split_bf16(x)
    return jnp.dot(jnp.concatenate([hi, lo], axis=1), m2_bf16, preferred_element_type=F32)


def _dot_split_rhs(m_bf16, x):
    hi, lo = _split_bf16(x)
    out = jnp.dot(m_bf16, jnp.concatenate([hi, lo], axis=1), preferred_element_type=F32)
    return out[:, :LANES] + out[:, LANES:]


def _rms_kernel(x_ref, g_ref, o_ref):
    x = x_ref[...]
    ms = jnp.sum(x * x, axis=-1, keepdims=True) * (1.0 / x.shape[-1])
    o_ref[...] = ((x * lax.rsqrt(ms + RMS_EPS)) * g_ref[...]).astype(o_ref.dtype)


def rmsnorm_rows(x, g, *, tm, out_dtype):
    m, d = x.shape
    return pl.pallas_call(
        _rms_kernel,
        grid=(m // tm,),
        in_specs=[pl.BlockSpec((tm, d), lambda i: (i, 0)), pl.BlockSpec((1, d), lambda i: (0, 0))],
        out_specs=pl.BlockSpec((tm, d), lambda i: (i, 0)),
        out_shape=jax.ShapeDtypeStruct((m, d), out_dtype),
        compiler_params=_cparams(("parallel",)),
    )(x, g.reshape(1, d))


def _mm_kernel(x_ref, w_ref, o_ref):
    o_ref[...] = _dot(x_ref[...], w_ref[0]).astype(o_ref.dtype)


def _mm_res_kernel(x_ref, w_ref, r_ref, o_ref):
    o_ref[...] = (r_ref[...] + _dot(x_ref[...], w_ref[0])).astype(o_ref.dtype)


def _layer_spec(k, tn, layer, col_blk0=0):
    return pl.BlockSpec((1, k, tn), lambda i, j: (layer, 0, col_blk0 + j))


def matmul(x, w, *, tm, tn, layer=0, col0=0, n_cols=None, out_dtype=F32, residual=None):
    m, k = x.shape
    n = w.shape[2] if n_cols is None else n_cols
    assert m % tm == 0 and n % tn == 0 and col0 % tn == 0, (m, tm, n, tn, col0)
    in_specs = [pl.BlockSpec((tm, k), lambda i, j: (i, 0)), _layer_spec(k, tn, layer, col0 // tn)]
    args = [x, w]
    kern = _mm_kernel
    if residual is not None:
        in_specs.append(pl.BlockSpec((tm, tn), lambda i, j: (i, j)))
        args.append(residual)
        kern = _mm_res_kernel
    return pl.pallas_call(
        kern,
        grid=(m // tm, n // tn),
        in_specs=in_specs,
        out_specs=pl.BlockSpec((tm, tn), lambda i, j: (i, j)),
        out_shape=jax.ShapeDtypeStruct((m, n), out_dtype),
        compiler_params=_cparams(("parallel", "arbitrary")),
    )(*args)


def _mm_nt_kernel(x_ref, wt_ref, o_ref):
    o_ref[...] = _dot_nt(x_ref[...], wt_ref[0]).astype(o_ref.dtype)


def matmul_nt(x, wt, *, tm, tn, layer, row0, n_cols, out_dtype=F32):
    m, k = x.shape
    assert m % tm == 0 and n_cols % tn == 0 and row0 % 8 == 0 and row0 + n_cols <= wt.shape[1]
    return pl.pallas_call(
        _mm_nt_kernel,
        grid=(m // tm, n_cols // tn),
        in_specs=[pl.BlockSpec((tm, k), lambda i, j: (i, 0)),
                  pl.BlockSpec((pl.Element(1), pl.Element(tn), pl.Element(k)),
                               lambda i, j: (layer, (row0 // 8 + j * (tn // 8)) * 8, 0))],
        out_specs=pl.BlockSpec((tm, tn), lambda i, j: (i, j)),
        out_shape=jax.ShapeDtypeStruct((m, n_cols), out_dtype),
        compiler_params=_cparams(("parallel", "arbitrary")),
    )(x, wt)


def _mix_kernel(ya_ref, yb_ref, wa_ref, wb_ref, ga_ref, gb_ref, o_ref):
    pa = _dot(ya_ref[...], wa_ref[0])
    pb = _dot(yb_ref[...], wb_ref[0])
    o_ref[...] = (jax.nn.sigmoid(ga_ref[...]) * pa + jax.nn.sigmoid(gb_ref[...]) * pb).astype(o_ref.dtype)


def branch_mix(ya, yb, wa, wb, gates, *, tm, tn, layer):
    m, k = ya.shape
    n = wa.shape[2]
    assert m % tm == 0 and n % tn == 0, (m, tm, n, tn)
    nb = n // tn
    return pl.pallas_call(
        _mix_kernel,
        grid=(m // tm, nb),
        in_specs=[pl.BlockSpec((tm, k), lambda i, j: (i, 0)),
                  pl.BlockSpec((tm, k), lambda i, j: (i, 0)),
                  _layer_spec(k, tn, layer),
                  _layer_spec(k, tn, layer),
                  pl.BlockSpec((tm, tn), lambda i, j: (i, j)),
                  pl.BlockSpec((tm, tn), lambda i, j: (i, j + nb))],
        out_specs=pl.BlockSpec((tm, tn), lambda i, j: (i, j)),
        out_shape=jax.ShapeDtypeStruct((m, n), BF16),
        compiler_params=_cparams(("parallel", "arbitrary")),
    )(ya, yb, wa, wb, gates, gates)


def _shift_rows(x, first_rows, n):
    rolled = pltpu.roll(x, n, 0)
    row = lax.broadcasted_iota(I32, x.shape, 0)
    out = rolled
    for j in range(n):
        out = jnp.where(row == j, first_rows[j:j + 1, :], out)
    return out


def _rw_prep_kernel(x_ref, sp_ref, mu_ref, w0_ref, w2_ref, a0_ref, a2_ref, g2_ref, kk_ref, ka_ref,
                    r_o, lw_o, k_o, v_o, kk_o, a_o, g_o, carry):
    t = pl.program_id(1)
    tt = x_ref.shape[0]

    @pl.when(t == 0)
    def _():
        carry[...] = sp_ref[0]

    x = x_ref[...]
    xprev = _shift_rows(x, carry[...], 1)
    carry[...] = x[tt - 1:tt, :]
    xs = x + (xprev - x) * mu_ref[...]
    w = RW_WIDTH
    r = xs[:, 0:w]
    k = xs[:, w:2 * w]
    v = xs[:, 2 * w:3 * w]
    wd = xs[:, 3 * w:3 * w + W_LORA]
    ad = xs[:, 3 * w + W_LORA:3 * w + W_LORA + A_LORA]
    gd = xs[:, 3 * w + W_LORA + A_LORA:RW_PAD]
    z = -(w0_ref[...] + _dot(jnp.tanh(wd), w2_ref[...]))
    softplus = jnp.maximum(z, 0.0) + jnp.log(1.0 + jnp.exp(-jnp.abs(z)))
    w_log = -softplus - 0.5
    a = jax.nn.sigmoid(a0_ref[...] + _dot(ad, a2_ref[...]))
    r_o[...] = r
    lw_o[...] = -jnp.exp(w_log)
    k_o[...] = k * (1.0 + (a - 1.0) * ka_ref[...])
    v_o[...] = v
    kk_o[...] = k * kk_ref[...]
    a_o[...] = a
    g_o[...] = _dot(jax.nn.sigmoid(gd), g2_ref[...])


def rw_prep(rw, shift_prev, mu, w0, w2, a0, a2, g2, k_k, k_a, *, nseq, seqlen, tt):
    nt = seqlen // tt
    row = lambda b, t: (b * nt + t, 0)
    const = lambda b, t: (0, 0)
    vec = lambda n: pl.BlockSpec((1, n), const)
    out_sds = jax.ShapeDtypeStruct((nseq * seqlen, RW_WIDTH), F32)
    return pl.pallas_call(
        _rw_prep_kernel,
        grid=(nseq, nt),
        in_specs=[pl.BlockSpec((tt, RW_PAD), row),
                  pl.BlockSpec((1, 1, RW_PAD), lambda b, t: (b, 0, 0)),
                  vec(RW_PAD), vec(RW_WIDTH),
                  pl.BlockSpec((W_LORA, RW_WIDTH), const), vec(RW_WIDTH),
                  pl.BlockSpec((A_LORA, RW_WIDTH), const),
                  pl.BlockSpec((G_PAD, RW_WIDTH), const), vec(RW_WIDTH), vec(RW_WIDTH)],
        out_specs=[pl.BlockSpec((tt, RW_WIDTH), row)] * 7,
        out_shape=[out_sds] * 7,
        scratch_shapes=[pltpu.VMEM((1, RW_PAD), F32)],
        compiler_params=_cparams(("arbitrary", "arbitrary"), 48),
    )(rw, shift_prev, mu, w0, w2, a0, a2, g2, k_k, k_a)


def _pad_rows(x, rows):
    if x.shape[0] == rows:
        return x
    return jnp.concatenate([x, jnp.zeros((rows - x.shape[0], x.shape[1]), x.dtype)], axis=0)


def _scan_kernel(r_ref, lw_ref, k_ref, v_ref, kk_ref, a_ref, g_ref, rk_ref, lnw_ref, lnb_ref, s0_ref,
                 y_ref, so_ref, s_scr, *, chunk, hpb):
    c = chunk
    c2 = 2 * c
    ci = pl.program_id(2)

    @pl.when(ci == 0)
    def _():
        s_scr[...] = s0_ref[0]

    lane = lax.broadcasted_iota(I32, (c, LANES), 1)
    m0 = lane < RW_HEAD
    ei = lax.broadcasted_iota(I32, (LANES, LANES), 0)
    ej = lax.broadcasted_iota(I32, (LANES, LANES), 1)
    same_head = lax.shift_right_logical(ei, 6) == lax.shift_right_logical(ej, 6)
    e_sum = jnp.where(same_head, 1.0, 0.0).astype(F32)
    e_mean = e_sum * (1.0 / RW_HEAD)
    ti = lax.broadcasted_iota(I32, (c, c), 0)
    tj = lax.broadcasted_iota(I32, (c, c), 1)
    tril_incl = jnp.where(ti >= tj, 1.0, 0.0).astype(F32)
    ri = lax.broadcasted_iota(I32, (c2, LANES), 0)
    cj = lax.broadcasted_iota(I32, (c2, LANES), 1)
    rl = jnp.where(ri >= c, ri - c, ri)
    cl = jnp.where(cj >= c, cj - c, cj)
    same_blk = (jnp.where(ri >= c, 1, 0) == jnp.where(cj >= c, 1, 0)) & (cj < c2)
    strict = same_blk & (rl > cl)
    incl = same_blk & (rl >= cl)
    zeros_c = jnp.zeros((c, LANES), F32)
    n_steps = max(1, (c - 1).bit_length())

    pairs = range(hpb)
    sls = [slice(p * LANES, (p + 1) * LANES) for p in pairs]
    each = lambda f, *cols: [f(*xs) for xs in zip(*cols)]
    cat = lambda *xs: jnp.concatenate(xs, axis=0)
    e_sum_b = jnp.concatenate([e_sum, e_sum], axis=0).astype(BF16)
    e_mean_b = jnp.concatenate([e_mean, e_mean], axis=0).astype(BF16)
    tril_b = tril_incl.astype(BF16)
    strict2 = jnp.concatenate([strict, strict], axis=1)
    incl2 = jnp.concatenate([incl, incl], axis=1)

    r = [r_ref[:, sl] for sl in sls]
    lw = [lw_ref[:, sl] for sl in sls]
    k = [k_ref[:, sl] for sl in sls]
    v = [v_ref[:, sl] for sl in sls]
    kkr = [kk_ref[:, sl] for sl in sls]
    a = [a_ref[:, sl] for sl in sls]
    s = [s_scr[p] for p in pairs]

    ss = each(lambda x: _dot_split(x * x, e_sum_b), kkr)
    kk = each(lambda x, q: x / jnp.maximum(jnp.sqrt(q), 1e-12), kkr, ss)
    cum = each(lambda x: _dot_split_rhs(tril_b, x), lw)
    p_incl = each(jnp.exp, cum)
    p_excl = each(lambda x, y_: jnp.exp(x - y_), cum, lw)
    p_inv = each(lambda x: jnp.exp(-x), cum)
    at = each(lambda x, y_: -x * y_, kk, p_excl)
    rt = each(lambda x, y_: x * y_, r, p_incl)
    bt = each(lambda x, y_, z: x * y_ * z, kk, a, p_inv)
    kt = each(lambda x, y_: x * y_, k, p_inv)

    lhs4 = each(lambda x, y_: cat(jnp.where(m0, x, 0.0), jnp.where(m0, 0.0, x),
                                  jnp.where(m0, y_, 0.0), jnp.where(m0, 0.0, y_)), at, rt)
    m1 = each(lambda x, yb, yk: _dot_nt(x, cat(_pad_rows(cat(yb, yb), LANES), _pad_rows(cat(yk, yk), LANES))),
              lhs4, bt, kt)
    m1s = each(lambda x: jnp.where(strict2, x[:c2], 0.0), m1)
    m1i = each(lambda x: jnp.where(incl2, x[c2:], 0.0), m1)
    m2 = each(lambda x, y_, z: _dot_nt(cat(x, y_), z), at, rt, s)
    v2 = each(lambda x: _pad_rows(cat(x, x), LANES), v)
    rhs = each(lambda x, y_, z: cat(x[:c], x[:c]) + _dot(y_[:, LANES:], z), m2, m1s, v2)
    u = each(lambda x: _pad_rows(x, LANES), rhs)
    pw = each(lambda x: _pad_rows(x[:, :LANES], LANES), m1s)
    for step in range(n_steps):
        u = each(lambda x, y_: x + _dot(y_, x), u, pw)
        if step + 1 < n_steps:
            pw = each(lambda x: _dot(x, x), pw)
    u_pair = each(lambda x: jnp.where(m0, x[:c], x[c:c2]), u)
    u2 = each(lambda x: _pad_rows(cat(x, x), LANES), u_pair)
    yst = each(lambda x, uu, vv: _dot(x, cat(uu, vv)), m1i, u2, v2)
    y = each(lambda x, y_: x[c:] + jnp.where(m0, y_[:c], y_[c:]), m2, yst)

    p_end = each(lambda x: x[c - 1:c, :], p_incl)
    xt = each(lambda x, y_: _pad_rows(cat(x, y_), LANES), u_pair, v)
    yk = each(lambda x, y_, z: _pad_rows(cat(x * z, y_ * z), LANES), bt, kt, p_end)
    s_new = each(lambda x, z, xx, yy: x * z + _dot(xx.T, yy), s, p_end, xt, yk)
    for p in pairs:
        s_scr[p] = jnp.where(same_head, s_new[p], 0.0)

    mean = each(lambda x: _dot_split(x, e_mean_b), y)
    d = each(lambda x, y_: x - y_, y, mean)
    var = each(lambda x: _dot_split(x * x, e_mean_b), d)
    bonus = each(lambda x, y_, sl: _dot_split(x * y_ * rk_ref[:, sl], e_sum_b), r, k, sls)
    for p in pairs:
        sl = sls[p]
        yn = d[p] * lax.rsqrt(var[p] + GN_EPS) * lnw_ref[:, sl] + lnb_ref[:, sl]
        y_ref[:, sl] = ((yn + bonus[p] * v[p]) * g_ref[:, sl]).astype(y_ref.dtype)

    @pl.when(ci == pl.num_programs(2) - 1)
    def _():
        so_ref[0] = s_scr[...]


def rw_scan(r, lw, k, v, kk, a, g, r_k, ln_w, ln_b, s0, *, nseq, seqlen, chunk, hpb, y_dtype):
    nc = seqlen // chunk
    npair = RW_HEADS // 2
    width = hpb * LANES
    row = lambda b, h, t: (b * nc + t, h)
    par = lambda b, h, t: (0, h)
    st = lambda b, h, t: (b, h, 0, 0)
    kern = functools.partial(_scan_kernel, chunk=chunk, hpb=hpb)
    return pl.pallas_call(
        kern,
        grid=(nseq, npair // hpb, nc),
        in_specs=[pl.BlockSpec((chunk, width), row)] * 7 + [pl.BlockSpec((1, width), par)] * 3
                 + [pl.BlockSpec((1, hpb, LANES, LANES), st)],
        out_specs=[pl.BlockSpec((chunk, width), row), pl.BlockSpec((1, hpb, LANES, LANES), st)],
        out_shape=[jax.ShapeDtypeStruct((nseq * seqlen, RW_WIDTH), y_dtype),
                   jax.ShapeDtypeStruct((nseq, npair, LANES, LANES), F32)],
        scratch_shapes=[pltpu.VMEM((hpb, LANES, LANES), F32)],
        compiler_params=_cparams(("parallel", "parallel", "arbitrary"), 32),
    )(r, lw, k, v, kk, a, g, r_k, ln_w, ln_b, s0)


def _pair_states(s):
    n = s.shape[0]
    s = s.reshape(n, RW_HEADS // 2, 2, RW_HEAD, RW_HEAD)
    z = jnp.zeros_like(s[:, :, 0])
    top = jnp.concatenate([s[:, :, 0], z], axis=-1)
    bot = jnp.concatenate([z, s[:, :, 1]], axis=-1)
    return jnp.concatenate([top, bot], axis=-2)


def _unpair_states(sp):
    n = sp.shape[0]
    s0 = sp[:, :, :RW_HEAD, :RW_HEAD]
    s1 = sp[:, :, RW_HEAD:, RW_HEAD:]
    return jnp.stack([s0, s1], axis=2).reshape(n, RW_HEADS, RW_HEAD, RW_HEAD)


def _kidx_kernel(x_ref, g_ref, kn_ref, kd_ref):
    x = x_ref[...]
    lane = lax.broadcasted_iota(I32, x.shape, 1)
    xk = jnp.where(lane < IDX_DIM, x, 0.0)
    ms = jnp.sum(xk * xk, axis=-1, keepdims=True) * (1.0 / IDX_DIM)
    xd = xk + pltpu.roll(xk, IDX_DIM, 1)
    kd = (xd * lax.rsqrt(ms + RMS_EPS)) * g_ref[...]
    kd_ref[...] = kd
    kn_ref[...] = kd[:, :IDX_DIM]


def kidx_norm(idx_cols, g, *, tm):
    m = idx_cols.shape[0]
    g2 = jnp.concatenate([g, g]).reshape(1, LANES)
    return pl.pallas_call(
        _kidx_kernel,
        grid=(m // tm,),
        in_specs=[pl.BlockSpec((tm, LANES), lambda i: (i, IDX_HEADS * IDX_DIM // LANES)),
                  pl.BlockSpec((1, LANES), lambda i: (0, 0))],
        out_specs=[pl.BlockSpec((tm, IDX_DIM), lambda i: (i, 0)), pl.BlockSpec((tm, LANES), lambda i: (i, 0))],
        out_shape=[jax.ShapeDtypeStruct((m, IDX_DIM), F32), jax.ShapeDtypeStruct((m, LANES), F32)],
        compiler_params=_cparams(("parallel",), 16),
    )(idx_cols, g2)


def _sort_key(score):
    score = jnp.where(score == 0.0, 0.0, score)
    bits = pltpu.bitcast(score, I32)
    return bits ^ (lax.shift_right_arithmetic(bits, 31) & 0x7FFFFFFF)


def _count_lanes(acc):
    return jnp.dot(acc.astype(BF16), jnp.ones((LANES, LANES), BF16), preferred_element_type=F32)


def _topk_select(key_tile, cut_ref, *, rows, n_tiles, topk, col_bits):
    def count(pred):
        acc = jnp.zeros((rows, LANES), F32)
        for t in range(n_tiles):
            acc = acc + jnp.where(pred(key_tile(t), t), 1.0, 0.0)
        return _count_lanes(acc)

    def bit_step(it, thr):
        cand = thr + lax.shift_left(jnp.int32(1), 31 - it)
        cnt = count(lambda kt, t: kt >= cand)
        return jnp.where(cnt >= topk, cand, thr)

    thr = lax.fori_loop(0, 32, bit_step, jnp.full((rows, LANES), INT_MIN, I32))
    n_gt = count(lambda kt, t: kt > thr)
    n_ge = count(lambda kt, t: kt >= thr)
    excess = (n_ge > topk) & (thr > NEG_INF_KEY)
    cut_ref[...] = jnp.full((rows, LANES), BIG_COL, I32)
    lane = lax.broadcasted_iota(I32, (rows, LANES), 1)

    @pl.when(jnp.max(jnp.where(excess, 1.0, 0.0)) > 0.0)
    def _():
        need = topk - n_gt

        def col_step(it, cut):
            cand = cut + lax.shift_left(jnp.int32(1), col_bits - 1 - it)
            cnt = count(lambda kt, t: (kt == thr) & (lane + t * LANES < cand))
            return jnp.where(cnt < need, cand, cut)

        cut = lax.fori_loop(0, col_bits, col_step, jnp.zeros((rows, LANES), I32))
        cut_ref[...] = jnp.where(excess, cut, BIG_COL)

    return thr


def _selected(kt, t, thr, cut):
    lane = lax.broadcasted_iota(I32, kt.shape, 1)
    return ((kt > thr) | ((kt == thr) & (lane + t * LANES <= cut))) & (kt != NEG_INF_KEY)


def _idx_scores(q_ref, wq_ref, kd, *, rows):
    lane = lax.broadcasted_iota(I32, (rows, LANES), 1)
    lo = lane < IDX_DIM
    k_hi, k_lo = _split_bf16(kd)
    k_cat = jnp.concatenate([k_hi, k_lo], axis=1)
    zero = jnp.zeros((rows, LANES), BF16)
    score = jnp.zeros((rows, kd.shape[0]), F32)
    for hp in range(IDX_HEADS // 2):
        qp = q_ref[:, hp * LANES:(hp + 1) * LANES]
        q_hi = qp.astype(BF16)
        q_lo = pltpu.roll(qp - q_hi.astype(F32), IDX_DIM, 1).astype(BF16)
        for half in range(2):
            h = 2 * hp + half
            own = lo if half == 0 else jnp.logical_not(lo)
            lhs = jnp.concatenate([jnp.where(own, q_hi, q_lo), jnp.where(own, q_hi, zero)], axis=1)
            d = lax.dot_general(lhs, k_cat, (((1,), (1,)), ((), ())), preferred_element_type=F32)
            wh = wq_ref[:, IDX_DIM + h:IDX_DIM + h + 1] * IDX_SCALE
            score = score + wh * jnp.maximum(d, 0.0)
    return score


def _dot_nt3(a, b_hi, b_lo):
    a_hi, a_lo = _split_bf16(a)
    nt = lambda x, y: lax.dot_general(x, y, (((1,), (1,)), ((), ())), preferred_element_type=F32)
    return nt(a_hi, b_hi) + (nt(a_hi, b_lo) + nt(a_lo, b_hi))


def _prompt_index_kernel(q_ref, wq_ref, kd_ref, m_ref, key_scr, cut_scr, *, tq, q0, kv, n_tiles, topk):
    cols = n_tiles * LANES
    q_pos = q0 + lax.broadcasted_iota(I32, (tq, cols), 0)
    score = _idx_scores(q_ref, wq_ref, kd_ref[0], rows=tq)
    col = lax.broadcasted_iota(I32, (tq, cols), 1)
    key_scr[...] = _sort_key(jnp.where(col <= q_pos, score, -jnp.inf))
    key_tile = lambda t: key_scr[:, t * LANES:(t + 1) * LANES]
    thr = _topk_select(key_tile, cut_scr, rows=tq, n_tiles=n_tiles, topk=topk, col_bits=12)
    cut = cut_scr[...]
    for t in range(n_tiles):
        width = min(LANES, kv - t * LANES)
        sel = jnp.where(_selected(key_tile(t), t, thr, cut), 1.0, 0.0).astype(m_ref.dtype)
        m_ref[:, t * LANES:t * LANES + width] = sel[:, :width]


def prompt_index_mask(idx_cols, kdup_pad, *, nseq, seqlen, tq, qi, topk):
    nq = seqlen // tq
    kv = (qi + 1) * tq
    n_tiles = -(-kv // LANES)
    cols = n_tiles * LANES
    assert cols <= kdup_pad.shape[1]
    kern = functools.partial(_prompt_index_kernel, tq=tq, q0=qi * tq, kv=kv, n_tiles=n_tiles, topk=topk)
    return pl.pallas_call(
        kern,
        grid=(nseq,),
        in_specs=[pl.BlockSpec((tq, IDX_HEADS * IDX_DIM), lambda b: (b * nq + qi, 0)),
                  pl.BlockSpec((tq, LANES), lambda b: (b * nq + qi, IDX_HEADS * IDX_DIM // LANES)),
                  pl.BlockSpec((1, cols, LANES), lambda b: (b, 0, 0))],
        out_specs=pl.BlockSpec((tq, kv), lambda b: (b, 0)),
        out_shape=jax.ShapeDtypeStruct((nseq * tq, kv), BF16),
        scratch_shapes=[pltpu.VMEM((tq, cols), I32), pltpu.VMEM((tq, LANES), I32)],
        compiler_params=_cparams(("parallel",)),
    )(idx_cols, idx_cols, kdup_pad)


def _prompt_attn_kernel(q_ref, k_ref, v_ref, m_ref, o_ref):
    logits = _dot_nt(q_ref[...] * ATT_SCALE, k_ref[0])
    logits = jnp.where(m_ref[...] > 0, logits, -jnp.inf)
    mx = jnp.max(logits, axis=-1, keepdims=True)
    e = jnp.exp(logits - mx)
    inv = 1.0 / jnp.sum(e, axis=-1, keepdims=True)
    o_ref[0] = (_dot(e, v_ref[0]) * inv).astype(o_ref.dtype)


def prompt_attention(qkv, mask, *, nseq, seqlen, tq, qi):
    nq = seqlen // tq
    kv = (qi + 1) * tq
    qkv3 = qkv.reshape(nseq, seqlen, 3 * ATT_WIDTH)
    return pl.pallas_call(
        _prompt_attn_kernel,
        grid=(nseq, ATT_HEADS),
        in_specs=[pl.BlockSpec((tq, ATT_HEAD), lambda b, h: (b * nq + qi, h)),
                  pl.BlockSpec((1, kv, ATT_HEAD), lambda b, h: (b, 0, ATT_HEADS + h)),
                  pl.BlockSpec((1, kv, ATT_HEAD), lambda b, h: (b, 0, 2 * ATT_HEADS + h)),
                  pl.BlockSpec((tq, kv), lambda b, h: (b, 0))],
        out_specs=pl.BlockSpec((1, tq, ATT_HEAD), lambda b, h: (b, 0, h)),
        out_shape=jax.ShapeDtypeStruct((nseq, tq, ATT_WIDTH), BF16),
        compiler_params=_cparams(("parallel", "arbitrary")),
    )(qkv, qkv3, qkv3, mask)


PAGES_PER_STEP = 8


def _sample_score_kernel(pt_ref, q_ref, w_ref, *rest, dsq):
    page_refs, o_ref = rest[:-1], rest[-1]
    q = q_ref[0]
    w = w_ref[0]
    q_hi, q_lo = _split_bf16(q)
    mm = lambda a, b_: jnp.dot(a, b_, preferred_element_type=F32)
    for j, page_ref in enumerate(page_refs):
        p_hi, p_lo = _split_bf16(page_ref[0, 0])
        d = jnp.maximum(mm(q_hi, p_hi) + (mm(q_hi, p_lo) + mm(q_lo, p_hi)), 0.0) * w
        o_ref[0, :, j * PAGE_SIZE:(j + 1) * PAGE_SIZE] = jnp.sum(d.reshape(IDX_HEADS, dsq, PAGE_SIZE), axis=0)


def sample_page_scores(page_table, qh, wh, cache_kidx, layer, *, nb, dsq, n_pages):
    kern = functools.partial(_sample_score_kernel, dsq=dsq)
    rows = IDX_HEADS * dsq
    pps = PAGES_PER_STEP
    assert n_pages % pps == 0

    cache_t = jnp.swapaxes(cache_kidx, 2, 3)

    def page_spec(j):
        return pl.BlockSpec((1, 1, IDX_DIM, PAGE_SIZE),
                            lambda b, p, pt: (layer, pt[b * n_pages + p * pps + j], 0, 0))

    grid_spec = pltpu.PrefetchScalarGridSpec(
        num_scalar_prefetch=1,
        grid=(nb, n_pages // pps),
        in_specs=[pl.BlockSpec((1, rows, IDX_DIM), lambda b, p, pt: (b, 0, 0)),
                  pl.BlockSpec((1, rows, 1), lambda b, p, pt: (b, 0, 0))] + [page_spec(j) for j in range(pps)],
        out_specs=pl.BlockSpec((1, dsq, pps * PAGE_SIZE), lambda b, p, pt: (b, 0, p)),
    )
    return pl.pallas_call(
        kern,
        grid_spec=grid_spec,
        out_shape=jax.ShapeDtypeStruct((nb, dsq, n_pages * PAGE_SIZE), F32),
        compiler_params=_cparams(("parallel", "arbitrary"), 16),
    )(page_table.reshape(-1), qh, wh, *([cache_t] * pps))


def _sample_select_kernel(sc_ref, q_ref, w_ref, kn_ref, idx_ref, key_scr, keyq_scr, rank_scr, cut_scr,
                          *, dsq, past, topk, nt_pad):
    n_past_tiles = past // LANES
    n_tiles = n_past_tiles + 1
    keyq_scr[...] = jnp.full(keyq_scr.shape, NEG_INF_KEY, I32)

    def put_tile(t, keys):
        key_scr[t] = keys
        for q in range(dsq):
            keyq_scr[q, t:t + 1, :] = keys[q:q + 1, :]

    for t in range(n_past_tiles):
        put_tile(t, _sort_key(sc_ref[0, :, t * LANES:(t + 1) * LANES]))
    kn_hi, kn_lo = _split_bf16(kn_ref[0])
    d = jnp.maximum(_dot_nt3(q_ref[0], kn_hi, kn_lo), 0.0) * w_ref[0]
    s_new = jnp.sum(d.reshape(IDX_HEADS, dsq, LANES), axis=0)
    qrow = lax.broadcasted_iota(I32, (dsq, LANES), 0)
    jcol = lax.broadcasted_iota(I32, (dsq, LANES), 1)
    put_tile(n_past_tiles, _sort_key(jnp.where(jcol <= qrow, s_new, -jnp.inf)))
    thr = _topk_select(lambda t: key_scr[t], cut_scr, rows=dsq, n_tiles=n_tiles, topk=topk, col_bits=15)
    cut = cut_scr[...]

    li = lax.broadcasted_iota(I32, (LANES, LANES), 0)
    lj = lax.broadcasted_iota(I32, (LANES, LANES), 1)
    upper = jnp.where(li < lj, 1.0, 0.0).astype(BF16)
    ones = jnp.ones((LANES, LANES), BF16)
    ti = lax.broadcasted_iota(I32, (nt_pad, nt_pad), 0)
    tj = lax.broadcasted_iota(I32, (nt_pad, nt_pad), 1)
    earlier = jnp.where(tj < ti, 1.0, 0.0).astype(BF16)
    col = (lax.broadcasted_iota(I32, (nt_pad, LANES), 0) * LANES + lax.broadcasted_iota(I32, (nt_pad, LANES), 1))
    slot_f = lax.broadcasted_iota(I32, (topk, LANES), 0).astype(F32)
    lane_f = lax.broadcasted_iota(I32, (topk, LANES), 1).astype(F32)
    for q in range(dsq):
        kq = keyq_scr[q]
        thr_q = thr[q:q + 1]
        cut_q = cut[q:q + 1]
        sel = ((kq > thr_q) | ((kq == thr_q) & (col <= cut_q))) & (kq != NEG_INF_KEY)
        selb = jnp.where(sel, 1.0, 0.0).astype(BF16)
        within = jnp.dot(selb, upper, preferred_element_type=F32)
        totals = jnp.dot(selb, ones, preferred_element_type=F32)
        before = jnp.dot(earlier, totals.astype(BF16), preferred_element_type=F32)
        rank_scr[...] = jnp.where(sel, before + within, -1.0)

        def tile_step(t, acc):
            rk = jnp.broadcast_to(rank_scr[pl.ds(t, 1), :], (topk, LANES))
            return acc + jnp.where(rk == slot_f, lane_f + jnp.asarray(t * LANES).astype(F32), 0.0)

        acc = lax.fori_loop(0, n_tiles, tile_step, jnp.zeros((topk, LANES), F32))
        idx_ref[0, q] = jnp.sum(acc, axis=1, keepdims=True).astype(I32)


def sample_select(scores, qh, wh, kn_pad, *, nb, dsq, past, topk):
    rows = IDX_HEADS * dsq
    n_tiles = past // LANES + 1
    nt_pad = -(-n_tiles // LANES) * LANES
    kern = functools.partial(_sample_select_kernel, dsq=dsq, past=past, topk=topk, nt_pad=nt_pad)
    return pl.pallas_call(
        kern,
        grid=(nb,),
        in_specs=[pl.BlockSpec((1, dsq, past), lambda b: (b, 0, 0)),
                  pl.BlockSpec((1, rows, IDX_DIM), lambda b: (b, 0, 0)),
                  pl.BlockSpec((1, rows, 1), lambda b: (b, 0, 0)),
                  pl.BlockSpec((1, LANES, IDX_DIM), lambda b: (b, 0, 0))],
        out_specs=pl.BlockSpec((1, dsq, topk, 1), lambda b: (b, 0, 0, 0)),
        out_shape=jax.ShapeDtypeStruct((nb, dsq, topk, 1), I32),
        scratch_shapes=[pltpu.VMEM((n_tiles, dsq, LANES), I32), pltpu.VMEM((dsq, nt_pad, LANES), I32),
                        pltpu.VMEM((nt_pad, LANES), F32), pltpu.VMEM((dsq, LANES), I32)],
        compiler_params=_cparams(("parallel",), 32),
    )(scores, qh, wh, kn_pad)


GATHER_UNROLL = 8


def _sample_attn_kernel(idx_ref, pt_ref, q_ref, pos_ref, kn_ref, vn_ref, ck_ref, cv_ref, o_ref,
                        kbuf, vbuf, sem, *, layer, dsq, past, n_pages, topk):
    b = pl.program_id(0)

    def start_query(q, slot):
        def body(j8, pos_max):
            for jj in range(GATHER_UNROLL):
                j = j8 * GATHER_UNROLL + jj
                pos = idx_ref[(b * dsq + q) * topk + j]
                ppos = jnp.minimum(pos, past - 1)
                page = pt_ref[b * n_pages + lax.shift_right_logical(ppos, 7)]
                off = ppos & (PAGE_SIZE - 1)
                pltpu.make_async_copy(ck_ref.at[layer, page, off], kbuf.at[slot, :, j], sem.at[0, slot]).start()
                pltpu.make_async_copy(cv_ref.at[layer, page, off], vbuf.at[slot, :, j], sem.at[1, slot]).start()
                pos_max = jnp.maximum(pos_max, pos)
            return pos_max

        return lax.fori_loop(0, topk // GATHER_UNROLL, body, jnp.int32(0))

    def wait_query(slot):
        pltpu.make_async_copy(kbuf.at[1 - slot], kbuf.at[slot], sem.at[0, slot]).wait()
        pltpu.make_async_copy(vbuf.at[1 - slot], vbuf.at[slot], sem.at[1, slot]).wait()

    def patch_new_rows(q, slot):
        rel = pos_ref[0, q] - past
        lane = lax.broadcasted_iota(I32, (topk, LANES), 1)
        onehot = jnp.where(rel == lane, 1.0, 0.0).astype(BF16)
        is_new = rel >= 0
        pad = jnp.zeros((LANES - dsq, ATT_HEAD), F32)
        for h in range(ATT_HEADS):
            for buf, new_ref in ((kbuf, kn_ref), (vbuf, vn_ref)):
                rows = jnp.concatenate([new_ref[0, :, h, :], pad], axis=0)
                hi, lo = _split_bf16(rows)
                fix = (jnp.dot(onehot, hi, preferred_element_type=F32) + jnp.dot(onehot, lo, preferred_element_type=F32))
                buf[slot, h] = jnp.where(is_new, fix, buf[slot, h])

    pos_max = start_query(0, 0)
    for q in range(dsq):
        slot = q % 2
        next_max = start_query(q + 1, 1 - slot) if q + 1 < dsq else None
        wait_query(slot)

        @pl.when(pos_max >= past)
        def _():
            patch_new_rows(q, slot)

        heads = range(ATT_HEADS)
        qs = [jnp.broadcast_to(q_ref[0, q, h:h + 1, :], (8, ATT_HEAD)) for h in heads]
        logits = jnp.concatenate([_dot_nt(qs[h], kbuf[slot, h])[0:1] for h in heads], axis=0) * ATT_SCALE
        mx = jnp.max(logits, axis=-1, keepdims=True)
        e = jnp.exp(logits - mx)
        prob = e / jnp.sum(e, axis=-1, keepdims=True)
        outs = [_dot(jnp.broadcast_to(prob[h:h + 1], (8, topk)), vbuf[slot, h])[0:1] for h in heads]
        o_ref[0, q:q + 1, :] = jnp.concatenate(outs, axis=1)
        pos_max = next_max


def sample_attention(sel_idx, page_table, q, k_new, v_new, cache_k, cache_v, layer, *, nb, dsq, past, topk):
    n_pages = past // PAGE_SIZE
    assert topk % GATHER_UNROLL == 0
    kern = functools.partial(_sample_attn_kernel, layer=layer, dsq=dsq, past=past, n_pages=n_pages, topk=topk)
    new_spec = pl.BlockSpec((1, dsq, ATT_HEADS, ATT_HEAD), lambda b, idx, pt: (b, 0, 0, 0))
    grid_spec = pltpu.PrefetchScalarGridSpec(
        num_scalar_prefetch=2,
        grid=(nb,),
        in_specs=[new_spec,
                  pl.BlockSpec((1, dsq, topk, 1), lambda b, idx, pt: (b, 0, 0, 0)),
                  new_spec, new_spec,
                  pl.BlockSpec(memory_space=pl.ANY), pl.BlockSpec(memory_space=pl.ANY)],
        out_specs=pl.BlockSpec((1, dsq, ATT_WIDTH), lambda b, idx, pt: (b, 0, 0)),
        scratch_shapes=[pltpu.VMEM((2, ATT_HEADS, topk, ATT_HEAD), F32),
                        pltpu.VMEM((2, ATT_HEADS, topk, ATT_HEAD), F32),
                        pltpu.SemaphoreType.DMA((2, 2))],
    )
    return pl.pallas_call(
        kern,
        grid_spec=grid_spec,
        out_shape=jax.ShapeDtypeStruct((nb, dsq, ATT_WIDTH), F32),
        compiler_params=_cparams(("arbitrary",), 32),
    )(sel_idx.reshape(-1), page_table.reshape(-1), q, sel_idx, k_new, v_new, cache_k, cache_v)


def _conv_gate_kernel(ug_ref, uv_ref, pg_ref, pv_ref, wg_ref, wv_ref, bg_ref, bv_ref, o_ref, *, nb, seqlen):
    def conv(u, prev, w, bias):
        return w[0:1, :] * _shift_rows(u, prev, 2) + w[1:2, :] * _shift_rows(u, prev[1:2, :], 1) + w[2:3, :] * u + bias

    outs = []
    for s in range(nb):
        rows = slice(s * seqlen, (s + 1) * seqlen)
        gate = conv(ug_ref[rows, :], pg_ref[s], wg_ref[...], bg_ref[...])
        val = conv(uv_ref[rows, :], pv_ref[s], wv_ref[...], bv_ref[...])
        outs.append(gate * jax.nn.sigmoid(gate) * val)
    o_ref[...] = (outs[0] if nb == 1 else jnp.concatenate(outs, axis=0)).astype(o_ref.dtype)


def conv_gate(u, conv_prev, conv_w, conv_b, *, nseq, seqlen, nb, tc):
    m = nseq * seqlen
    half = D_FF // tc
    kern = functools.partial(_conv_gate_kernel, nb=nb, seqlen=seqlen)
    rows = nb * seqlen
    return pl.pallas_call(
        kern,
        grid=(nseq // nb, half),
        in_specs=[pl.BlockSpec((rows, tc), lambda b, j: (b, j)),
                  pl.BlockSpec((rows, tc), lambda b, j: (b, j + half)),
                  pl.BlockSpec((nb, CONV_W - 1, tc), lambda b, j: (b, 0, j)),
                  pl.BlockSpec((nb, CONV_W - 1, tc), lambda b, j: (b, 0, j + half)),
                  pl.BlockSpec((CONV_W, tc), lambda b, j: (0, j)),
                  pl.BlockSpec((CONV_W, tc), lambda b, j: (0, j + half)),
                  pl.BlockSpec((1, tc), lambda b, j: (0, j)),
                  pl.BlockSpec((1, tc), lambda b, j: (0, j + half))],
        out_specs=pl.BlockSpec((rows, tc), lambda b, j: (b, j)),
        out_shape=jax.ShapeDtypeStruct((m, D_FF), BF16),
        compiler_params=_cparams(("parallel", "arbitrary"), 48),
    )(u, u, conv_prev, conv_prev, conv_w, conv_w, conv_b, conv_b)


def _layer_weights(l, rw_mu, rw_g2):
    mu = jnp.pad(rw_mu[l], (0, RW_PAD - RW_COLS)).reshape(1, RW_PAD)
    g2 = jnp.pad(rw_g2[l], ((0, G_PAD - G_LORA), (0, 0)))
    return mu, g2


def _trunk_layer(x, l, p, lw, shift_prev, wkv_prev, conv_prev, attend, *, nseq, seqlen, tiles):
    m = nseq * seqlen
    mu, g2 = lw
    w_in_t = p['w_in_t']
    row = lambda a: a.reshape(1, -1)
    h = rmsnorm_rows(x, p['norm_mix'][l], tm=tiles['rms'], out_dtype=BF16)
    proj = lambda col0, n, tn: matmul_nt(h, w_in_t, layer=l, row0=col0, n_cols=n, tm=tiles['tm'], tn=tn)
    rw = proj(0, RW_PAD, tiles['tn_rw'])
    qkv = proj(RW_COLS, 3 * ATT_WIDTH, tiles['tn_qkv'])
    idx_cols = proj(RW_COLS + 3 * ATT_WIDTH, IDX_PAD, tiles['tn_idx'])
    gates = proj(RW_COLS + ATT_COLS, 2 * D_MODEL, tiles['tn_gate'])

    sp = jnp.pad(shift_prev, ((0, 0), (0, 0), (0, RW_PAD - RW_COLS)))
    r, lwd, k2, v, kk, a, g = rw_prep(rw, sp, mu, row(p['rw_w0'][l]), p['rw_w2'][l], row(p['rw_a0'][l]),
                                       p['rw_a2'][l], g2, row(p['rw_k_k'][l]), row(p['rw_k_a'][l]),
                                       nseq=nseq, seqlen=seqlen, tt=tiles['chunk'])
    y_a, s_pair = rw_scan(r, lwd, k2, v, kk, a, g, row(p['rw_r_k'][l]), row(p['rw_ln_w'][l]), row(p['rw_ln_b'][l]),
                          _pair_states(wkv_prev), nseq=nseq, seqlen=seqlen, chunk=tiles['chunk'], hpb=tiles['hpb'],
                          y_dtype=tiles['y_dtype'])
    y_a = y_a.astype(BF16)
    wkv_new = _unpair_states(s_pair)
    shift_new = rw.reshape(nseq, seqlen, RW_PAD)[:, seqlen - 1:, :RW_COLS]

    ki_n, ki_dup = kidx_norm(idx_cols, p['idx_k_norm'][l], tm=tiles['kidx'])
    y_b = attend(qkv, idx_cols, ki_n, ki_dup)

    mix = branch_mix(y_a, y_b, p['w_branch_a'], p['w_branch_b'], gates, tm=tiles['tm'], tn=tiles['tn_mix'], layer=l)
    x = matmul(mix, p['w_out'], layer=l, tm=tiles['tm'], tn=tiles['tn_out'], residual=x)

    h2 = rmsnorm_rows(x, p['norm_ffn'][l], tm=tiles['rms'], out_dtype=BF16)
    u = matmul(h2, p['w_up'], layer=l, tm=tiles['tm'], tn=tiles['tn_up'])
    act = conv_gate(u, conv_prev, p['conv_w'][l], row(p['conv_b'][l]), nseq=nseq, seqlen=seqlen,
                    nb=tiles['conv_nb'], tc=tiles['conv_tc'])
    u3 = u.reshape(nseq, seqlen, 2 * D_FF)
    conv_new = jnp.concatenate([conv_prev, u3], axis=1)[:, seqlen:] if seqlen < CONV_W - 1 else u3[:, seqlen - (CONV_W - 1):]
    x = matmul(act, p['w_down'], layer=l, tm=tiles['tm_down'], tn=tiles['tn_down'], residual=x)

    k_new = qkv[:, ATT_WIDTH:2 * ATT_WIDTH].reshape(nseq, seqlen, ATT_HEADS, ATT_HEAD)
    v_new = qkv[:, 2 * ATT_WIDTH:].reshape(nseq, seqlen, ATT_HEADS, ATT_HEAD)
    return x, (k_new, v_new, ki_n.reshape(nseq, seqlen, IDX_DIM), wkv_new, shift_new, conv_new)


PROMPT_TILES = dict(rms=688, tm=1376, tn_rw=256, tn_qkv=512, tn_idx=256, tn_gate=512, tn_mix=256, tn_out=256, tn_up=512,
                    tm_down=688, tn_down=256, chunk=48, hpb=16, kidx=688, conv_nb=1, conv_tc=256, tq=688, y_dtype=BF16)
SAMPLE_TILES = dict(rms=64, tm=64, tn_rw=768, tn_qkv=768, tn_idx=256, tn_gate=1024, tn_mix=1024, tn_out=1024, tn_up=512,
                    tm_down=64, tn_down=512, chunk=8, hpb=4, kidx=64, conv_nb=8, conv_tc=256, y_dtype=F32)


def kernel(x_prompt, x_sample, cache_k, cache_v, cache_kidx, state_wkv, state_shift, state_conv, page_table,
           meta_tokens, norm_mix, w_in, rw_mu, rw_w0, rw_w2, rw_a0, rw_a2, rw_g2, rw_k_k, rw_k_a, rw_r_k,
           rw_ln_w, rw_ln_b, idx_k_norm, w_branch_a, w_branch_b, w_out, norm_ffn, w_up, conv_w, conv_b,
           w_down, norm_final):
    p = {'norm_mix': norm_mix, 'rw_w0': rw_w0, 'rw_w2': rw_w2, 'rw_a0': rw_a0, 'rw_a2': rw_a2,
         'rw_k_k': rw_k_k, 'rw_k_a': rw_k_a, 'rw_r_k': rw_r_k.reshape(rw_r_k.shape[0], RW_WIDTH),
         'rw_ln_w': rw_ln_w, 'rw_ln_b': rw_ln_b, 'idx_k_norm': idx_k_norm, 'w_branch_a': w_branch_a,
         'w_branch_b': w_branch_b, 'w_out': w_out, 'norm_ffn': norm_ffn, 'w_up': w_up, 'conv_w': conv_w,
         'conv_b': conv_b, 'w_down': w_down, 'w_in_t': jnp.swapaxes(w_in, 1, 2)}
    depth = w_in.shape[0]
    nb_p, seq = x_prompt.shape[:2]
    tp = seq + N_META
    nb_s, ds = x_sample.shape[:2]
    n_pages = page_table.shape[1]
    past = n_pages * PAGE_SIZE

    meta = jnp.broadcast_to(meta_tokens[None], (nb_p, N_META, D_MODEL))
    xp = jnp.concatenate([meta, x_prompt], axis=1).reshape(nb_p * tp, D_MODEL)
    xs = x_sample.reshape(nb_s * ds, D_MODEL)

    topk_p = min(TOPK_MAX, tp // 4)
    topk_s = min(TOPK_MAX, (past + ds) // 4)
    lp = -(-tp // LANES) * LANES
    tq = PROMPT_TILES['tq']

    def prompt_attend(qkv, idx_cols, ki_n, ki_dup):
        kd = jnp.pad(ki_dup.reshape(nb_p, tp, LANES), ((0, 0), (0, lp - tp), (0, 0)))
        outs = []
        for qi in range(tp // tq):
            mask = prompt_index_mask(idx_cols, kd, nseq=nb_p, seqlen=tp, tq=tq, qi=qi, topk=topk_p)
            outs.append(prompt_attention(qkv, mask, nseq=nb_p, seqlen=tp, tq=tq, qi=qi))
        return jnp.concatenate(outs, axis=1).reshape(nb_p * tp, ATT_WIDTH)

    def make_sample_attend(l):
        def attend(qkv, idx_cols, ki_n, ki_dup):
            qi = idx_cols[:, :IDX_HEADS * IDX_DIM].reshape(nb_s, ds, IDX_HEADS, IDX_DIM)
            qh = jnp.swapaxes(qi, 1, 2).reshape(nb_s, IDX_HEADS * ds, IDX_DIM)
            wi = idx_cols[:, IDX_HEADS * IDX_DIM + IDX_DIM:IDX_HEADS * IDX_DIM + IDX_DIM + IDX_HEADS]
            wh = jnp.swapaxes(wi.reshape(nb_s, ds, IDX_HEADS), 1, 2).reshape(nb_s, IDX_HEADS * ds, 1) * IDX_SCALE
            scores = sample_page_scores(page_table, qh, wh, cache_kidx, l, nb=nb_s, dsq=ds, n_pages=n_pages)
            kn_pad = jnp.pad(ki_n.reshape(nb_s, ds, IDX_DIM), ((0, 0), (0, LANES - ds), (0, 0)))
            sel = sample_select(scores, qh, wh, kn_pad, nb=nb_s, dsq=ds, past=past, topk=topk_s)
            q = qkv[:, :ATT_WIDTH].reshape(nb_s, ds, ATT_HEADS, ATT_HEAD)
            k_new = qkv[:, ATT_WIDTH:2 * ATT_WIDTH].reshape(nb_s, ds, ATT_HEADS, ATT_HEAD)
            v_new = qkv[:, 2 * ATT_WIDTH:].reshape(nb_s, ds, ATT_HEADS, ATT_HEAD)
            y = sample_attention(sel, page_table, q, k_new, v_new, cache_k, cache_v, l,
                                 nb=nb_s, dsq=ds, past=past, topk=topk_s)
            return y.reshape(nb_s * ds, ATT_WIDTH).astype(BF16)
        return attend

    zero_shift = jnp.zeros((nb_p, 1, RW_COLS), F32)
    zero_wkv = jnp.zeros((nb_p, RW_HEADS, RW_HEAD, RW_HEAD), F32)
    zero_conv = jnp.zeros((nb_p, CONV_W - 1, 2 * D_FF), F32)
    sts_p, sts_s = [], []
    for l in range(depth):
        lw = _layer_weights(l, rw_mu, rw_g2)
        xp, st = _trunk_layer(xp, l, p, lw, zero_shift, zero_wkv, zero_conv, prompt_attend,
                              nseq=nb_p, seqlen=tp, tiles=PROMPT_TILES)
        sts_p.append(st)
        xs, st = _trunk_layer(xs, l, p, lw, state_shift[l], state_wkv[l], state_conv[l], make_sample_attend(l),
                              nseq=nb_s, seqlen=ds, tiles=SAMPLE_TILES)
        sts_s.append(st)

    y_prompt = rmsnorm_rows(xp, norm_final, tm=PROMPT_TILES['rms'], out_dtype=F32)
    y_prompt = y_prompt.reshape(nb_p, tp, D_MODEL)[:, N_META:]
    y_sample = rmsnorm_rows(xs, norm_final, tm=SAMPLE_TILES['rms'], out_dtype=F32).reshape(nb_s, ds, D_MODEL)
    stk = lambda sts, i: jnp.stack([s[i] for s in sts])
    return (y_prompt, y_sample,
            stk(sts_p, 0), stk(sts_p, 1), stk(sts_p, 2), stk(sts_p, 3), stk(sts_p, 4), stk(sts_p, 5),
            stk(sts_s, 0), stk(sts_s, 1), stk(sts_s, 2), stk(sts_s, 3), stk(sts_s, 4), stk(sts_s, 5))
```
